```python
import math
import jax, jax.numpy as jnp
from jax import lax
import numpy as np

D_MODEL = 2048
BATCH = 4
SEQ = 4096
DEPTH = 1

CHUNK = 128
A_WIDTH = D_MODEL
A_GROUPS = 8
A_GROUP_DIM = A_WIDTH // A_GROUPS
R_HEADS = 8
R_QK_DIM = D_MODEL // (2 * R_HEADS)
R_V_DIM = D_MODEL // R_HEADS
R_QK_WIDTH = R_HEADS * R_QK_DIM
R_V_WIDTH = R_HEADS * R_V_DIM
ROPE_BASE = 10000.0
LN_EPS = 1e-5
DEEPNORM_ALPHA = (2 * DEPTH) ** 0.25
DEEPNORM_BETA = (8 * DEPTH) ** -0.25

IN_WIDTHS = (A_WIDTH, A_WIDTH, A_WIDTH,
             R_QK_WIDTH, R_QK_WIDTH, R_V_WIDTH, R_V_WIDTH,
             D_MODEL, D_MODEL)
IN_WIDTH = sum(IN_WIDTHS)
IN_SPLITS = tuple(int(s) for s in np.cumsum(IN_WIDTHS)[:-1])

kernel_name = "hybrid_gmlp_retention_gated_deepnorm"


def standardize(x):
    xf = x.astype(jnp.float32)
    mu = jnp.mean(xf, axis=-1, keepdims=True)
    var = jnp.mean(jnp.square(xf - mu), axis=-1, keepdims=True)
    return ((xf - mu) * lax.rsqrt(var + LN_EPS)).astype(x.dtype)


def layer_norm(x, g, b):
    return standardize(x) * g + b


def rotary(x, positions):
    d = x.shape[-1]
    freqs = ROPE_BASE ** (-jnp.arange(0, d, 2, dtype=jnp.float32) / d)
    ang = positions.astype(jnp.float32)[:, None] * freqs[None, :]
    cos = jnp.cos(ang).astype(x.dtype)[None, :, None, :]
    sin = jnp.sin(ang).astype(x.dtype)[None, :, None, :]
    x1, x2 = x[..., : d // 2], x[..., d // 2:]
    return jnp.concatenate([x1 * cos - x2 * sin, x1 * sin + x2 * cos], axis=-1)


def spatial_gating(u, v, ln_g, ln_b, w_s, b_s):
    bsz, s, _ = v.shape
    n = s // CHUNK
    v = layer_norm(v, ln_g, ln_b).reshape(bsz, n, CHUNK, A_GROUPS, A_GROUP_DIM)
    causal = jnp.tril(jnp.ones((CHUNK, CHUNK), dtype=w_s.dtype))
    ws = w_s * causal[None]
    sv = jnp.einsum('gts,bnsgc->bntgc', ws, v) + b_s.T[None, None, :, :, None]
    return u * sv.reshape(bsz, s, A_WIDTH)


def retention(q, k, v):
    bsz, s, h, dk = q.shape
    dv = v.shape[-1]
    n = s // CHUNK
    dt = q.dtype
    log_gamma = jnp.log1p(-jnp.exp2(-5.0 - jnp.arange(h, dtype=jnp.float32)))
    idx = jnp.arange(CHUNK, dtype=jnp.float32)
    q = q.reshape(bsz, n, CHUNK, h, dk)
    k = (k * (dk ** -0.5)).reshape(bsz, n, CHUNK, h, dk)
    v = v.reshape(bsz, n, CHUNK, h, dv)
    rel = idx[:, None] - idx[None, :]
    decay = jnp.where(rel[None] >= 0,
                      jnp.exp(log_gamma[:, None, None] * jnp.maximum(rel, 0.0)[None]), 0.0).astype(dt)
    scores = jnp.einsum('bnthd,bnshd->bnhts', q, k) * decay[None, None]
    inner = jnp.einsum('bnhts,bnshe->bnthe', scores, v)
    w_state = jnp.exp(log_gamma[None, :] * (CHUNK - 1 - idx)[:, None]).astype(dt)
    kv = jnp.einsum('bnshd,bnshe->bnhde', k * w_state[None, None, :, :, None], v)
    chunk_decay = jnp.exp(log_gamma * CHUNK).astype(kv.dtype)[None, :, None, None]

    def step(state, kv_i):
        return state * chunk_decay + kv_i, state

    init = jnp.zeros((bsz, h, dk, dv), dtype=kv.dtype)
    _, r_prev = lax.scan(step, init, jnp.moveaxis(kv, 1, 0))
    r_prev = jnp.moveaxis(r_prev, 0, 1)
    cross_decay = jnp.exp(log_gamma[None, :] * (idx + 1.0)[:, None]).astype(dt)
    cross = jnp.einsum('bnthd,bnhde->bnthe', q, r_prev) * cross_decay[None, None, :, :, None]
    out = standardize(inner + cross)
    return out.reshape(bsz, s, h * dv)


def setup_inputs(seed: int = 0) -> dict:
    key = jax.random.key(seed)
    ks = jax.random.split(key, 13)
    f32 = jnp.float32
    nrm = lambda k, shape: jax.random.normal(k, shape, dtype=f32)
    x = nrm(ks[0], (BATCH, SEQ, D_MODEL))
    w_in = nrm(ks[1], (DEPTH, D_MODEL, IN_WIDTH)) * D_MODEL ** -0.5
    b_gate = 0.01 * nrm(ks[2], (DEPTH, 2 * D_MODEL))
    ln_v_g = 1.0 + 0.02 * nrm(ks[3], (DEPTH, A_WIDTH))
    ln_v_b = 0.02 * nrm(ks[4], (DEPTH, A_WIDTH))
    w_s = nrm(ks[5], (DEPTH, A_GROUPS, CHUNK, CHUNK)) * CHUNK ** -0.5
    b_s = 1.0 + 0.02 * nrm(ks[6], (DEPTH, A_GROUPS, CHUNK))
    w_oa = nrm(ks[7], (DEPTH, A_WIDTH, D_MODEL)) * (A_WIDTH ** -0.5) * DEEPNORM_BETA
    w_ob = nrm(ks[8], (DEPTH, R_V_WIDTH, D_MODEL)) * (R_V_WIDTH ** -0.5) * DEEPNORM_BETA
    w_out = nrm(ks[9], (DEPTH, D_MODEL, D_MODEL)) * (D_MODEL ** -0.5) * DEEPNORM_BETA
    ln_g = 1.0 + 0.02 * nrm(ks[10], (DEPTH, D_MODEL))
    ln_b = 0.02 * nrm(ks[11], (DEPTH, D_MODEL))
    return {"x": x, "w_in": w_in, "b_gate": b_gate, "ln_v_g": ln_v_g, "ln_v_b": ln_v_b,
            "w_s": w_s, "b_s": b_s, "w_oa": w_oa, "w_ob": w_ob, "w_out": w_out,
            "ln_g": ln_g, "ln_b": ln_b}


def reference(x, w_in, b_gate, ln_v_g, ln_v_b, w_s, b_s, w_oa, w_ob, w_out, ln_g, ln_b):
    bsz, s, _ = x.shape
    positions = jnp.arange(s, dtype=jnp.int32)
    for l in range(DEPTH):
        h = x @ w_in[l]
        a_u, a_v, a_z, r_q, r_k, r_v, r_z, g_a, g_b = jnp.split(h, IN_SPLITS, axis=-1)
        ya = spatial_gating(jax.nn.gelu(a_u), jax.nn.gelu(a_v), ln_v_g[l], ln_v_b[l], w_s[l], b_s[l])
        ya = (ya * jax.nn.silu(a_z)) @ w_oa[l]
        q = rotary(r_q.reshape(bsz, s, R_HEADS, R_QK_DIM), positions)
        k = rotary(r_k.reshape(bsz, s, R_HEADS, R_QK_DIM), positions)
        ret = retention(q, k, r_v.reshape(bsz, s, R_HEADS, R_V_DIM))
        yb = (ret * jax.nn.silu(r_z)) @ w_ob[l]
        ga = jax.nn.sigmoid(g_a + b_gate[l, :D_MODEL])
        gb = jax.nn.sigmoid(g_b + b_gate[l, D_MODEL:])
        out = (ga * ya + gb * yb) @ w_out[l]
        x = layer_norm(DEEPNORM_ALPHA * x + out, ln_g[l], ln_b[l])
    return x
```

```python
import functools

import numpy as np
import jax
import jax.numpy as jnp
from jax import lax
from jax.experimental import pallas as pl
from jax.experimental.pallas import tpu as pltpu

D_MODEL = 2048
CHUNK = 128
A_GROUPS = 8
A_GROUP_DIM = D_MODEL // A_GROUPS
R_HEADS = 8
R_QK_DIM = D_MODEL // (2 * R_HEADS)
R_V_DIM = D_MODEL // R_HEADS
R_QK_WIDTH = R_HEADS * R_QK_DIM
ROPE_BASE = 10000.0
LN_EPS = 1e-5
DEPTH = 1
DEEPNORM_ALPHA = (2 * DEPTH) ** 0.25
N_COL_GROUPS = 8
IN_WIDTH = N_COL_GROUPS * D_MODEL

V7X_VMEM_BYTES = 64 * 1024 * 1024

PROJ_ROWS = 512
MIX_ROWS = 256
OUT_ROWS = 256

F32 = jnp.float32
BF16 = jnp.bfloat16


def _gelu_tanh(x):
    c = np.float32(np.sqrt(2.0 / np.pi))
    return 0.5 * x * (1.0 + jnp.tanh(c * (x + np.float32(0.044715) * (x * x * x))))


def _sigmoid(x):
    return 0.5 * (jnp.tanh(0.5 * x) + 1.0)


def _silu(x):
    return x * _sigmoid(x)


def _standardize_rows(x):
    mu = jnp.mean(x, axis=-1, keepdims=True)
    d = x - mu
    var = jnp.mean(d * d, axis=-1, keepdims=True)
    return d * lax.rsqrt(var + LN_EPS)


def _proj_kernel(x_ref, w_ref, lng_ref, lnb_ref, bg_ref, cos_ref, sin_ref, o_ref):
    j = pl.program_id(0)

    def mm():
        return jnp.dot(x_ref[...], w_ref[...], preferred_element_type=F32)

    @pl.when((j == 0))
    def _():
        o_ref[...] = _gelu_tanh(mm()).astype(BF16)

    @pl.when(j == 1)
    def _():
        g = _standardize_rows(_gelu_tanh(mm()))
        o_ref[...] = (g * lng_ref[...] + lnb_ref[...]).astype(BF16)

    @pl.when((j == 2) | (j == 5))
    def _():
        o_ref[...] = _silu(mm()).astype(BF16)

    @pl.when(j == 3)
    def _():
        acc = mm()
        cos = cos_ref[...]
        sin = sin_ref[...]
        for c in range(2 * R_HEADS):
            blk = acc[:, c * R_QK_DIM:(c + 1) * R_QK_DIM]
            rot = blk * cos + pltpu.roll(blk, R_QK_DIM // 2, 1) * sin
            if c >= R_HEADS:
                rot = rot * np.float32(R_QK_DIM ** -0.5)
            o_ref[:, c * R_QK_DIM:(c + 1) * R_QK_DIM] = rot.astype(BF16)

    @pl.when(j == 4)
    def _():
        o_ref[...] = mm().astype(BF16)

    @pl.when(j == 6)
    def _():
        o_ref[...] = _sigmoid(mm() + bg_ref[0:1, :]).astype(BF16)

    @pl.when(j == 7)
    def _():
        o_ref[...] = _sigmoid(mm() + bg_ref[1:2, :]).astype(BF16)


def _project(xb, wb, ln_v_g, ln_v_b, b_gate2, cos_t, sin_t, seq):
    tokens = xb.shape[0]
    bm = PROJ_ROWS
    n_i = tokens // bm
    pos_blocks = seq // bm
    vmem = 2 * (bm * D_MODEL * 2 + D_MODEL * D_MODEL * 2 + bm * D_MODEL * 2) + 4 * bm * D_MODEL * 4
    return pl.pallas_call(
        _proj_kernel,
        grid=(N_COL_GROUPS, n_i),
        in_specs=[
            pl.BlockSpec((bm, D_MODEL), lambda j, i: (i, 0)),
            pl.BlockSpec((D_MODEL, D_MODEL), lambda j, i: (0, j)),
            pl.BlockSpec((1, D_MODEL), lambda j, i: (0, 0)),
            pl.BlockSpec((1, D_MODEL), lambda j, i: (0, 0)),
            pl.BlockSpec((2, D_MODEL), lambda j, i: (0, 0)),
            pl.BlockSpec((bm, R_QK_DIM), lambda j, i: (i % pos_blocks, 0)),
            pl.BlockSpec((bm, R_QK_DIM), lambda j, i: (i % pos_blocks, 0)),
        ],
        out_specs=pl.BlockSpec((bm, D_MODEL), lambda j, i: (i, j)),
        out_shape=jax.ShapeDtypeStruct((tokens, IN_WIDTH), BF16),
        compiler_params=pltpu.CompilerParams(
            dimension_semantics=("arbitrary", "arbitrary"),
            vmem_limit_bytes=min(vmem + (8 << 20), V7X_VMEM_BYTES - (8 << 20))),
        name="proj",
    )(xb, wb, ln_v_g, ln_v_b, b_gate2, cos_t, sin_t)


def _mix_kernel(chunk_decay, u_ref, vn_ref, az_ref, qk_ref, v_ref, rz_ref,
                ws_ref, bs_ref, dec_ref, wst_ref, cd_ref, a_out, b_out, state):
    @pl.when(pl.program_id(1) == 0)
    def _():
        state[...] = jnp.zeros_like(state)

    rows = u_ref.shape[0]
    for c in range(rows // CHUNK):
        r = slice(c * CHUNK, (c + 1) * CHUNK)
        for g in range(A_GROUPS):
            cs = slice(g * A_GROUP_DIM, (g + 1) * A_GROUP_DIM)
            sv = jnp.dot(ws_ref[g], vn_ref[r, cs], preferred_element_type=F32) + bs_ref[g]
            a = u_ref[r, cs].astype(F32) * sv * az_ref[r, cs].astype(F32)
            a_out[r, cs] = a.astype(BF16)
        for h in range(R_HEADS):
            q = qk_ref[r, h * R_QK_DIM:(h + 1) * R_QK_DIM]
            k = qk_ref[r, R_QK_WIDTH + h * R_QK_DIM:R_QK_WIDTH + (h + 1) * R_QK_DIM]
            vs = slice(h * R_V_DIM, (h + 1) * R_V_DIM)
            v = v_ref[r, vs]
            s = lax.dot_general(q, k, (((1,), (1,)), ((), ())), preferred_element_type=F32)
            s = s * dec_ref[h]
            inner = jnp.dot(s.astype(BF16), v, preferred_element_type=F32)
            st = state[h]
            cross = jnp.dot(q, st.astype(BF16), preferred_element_type=F32) * cd_ref[h]
            o = _standardize_rows(inner + cross)
            b_out[r, vs] = (o * rz_ref[r, vs].astype(F32)).astype(BF16)
            kw = (k.astype(F32) * wst_ref[h]).astype(BF16)
            kv = lax.dot_general(kw, v, (((0,), (0,)), ((), ())), preferred_element_type=F32)
            state[h] = st * chunk_decay[h] + kv


def _mix(p, ws_m, bs3, decay, wst3, cd3, chunk_decay, batch, seq):
    tokens = p.shape[0]
    tb = MIX_ROWS
    n_s = seq // tb

    def col(jcol):
        return pl.BlockSpec((tb, D_MODEL), lambda b, s: (b * n_s + s, jcol))

    def whole(a):
        nd = a.ndim
        return pl.BlockSpec(a.shape, lambda b, s: (0,) * nd)

    out_spec = pl.BlockSpec((tb, D_MODEL), lambda b, s: (b * n_s + s, 0))
    vmem = 2 * 8 * tb * D_MODEL * 2 + 8 * (1 << 20)
    return pl.pallas_call(
        functools.partial(_mix_kernel, chunk_decay),
        grid=(batch, n_s),
        in_specs=[col(0), col(1), col(2), col(3), col(4), col(5),
                  whole(ws_m), whole(bs3), whole(decay), whole(wst3), whole(cd3)],
        out_specs=[out_spec, out_spec],
        out_shape=[jax.ShapeDtypeStruct((tokens, D_MODEL), BF16)] * 2,
        scratch_shapes=[pltpu.VMEM((R_HEADS, R_QK_DIM, R_V_DIM), F32)],
        compiler_params=pltpu.CompilerParams(
            dimension_semantics=("arbitrary", "arbitrary"),
            vmem_limit_bytes=vmem + (8 << 20)),
        name="mix",
    )(p, p, p, p, p, p, ws_m, bs3, decay, wst3, cd3)


def _out_kernel(a_ref, b_ref, ga_ref, gb_ref, x_ref, woa_ref, wob_ref, wout_ref,
                lng_ref, lnb_ref, o_ref):
    ya = jnp.dot(a_ref[...], woa_ref[...], preferred_element_type=F32)
    yb = jnp.dot(b_ref[...], wob_ref[...], preferred_element_type=F32)
    m = ga_ref[...].astype(F32) * ya + gb_ref[...].astype(F32) * yb
    out = jnp.dot(m.astype(BF16), wout_ref[...], preferred_element_type=F32)
    y = _standardize_rows(np.float32(DEEPNORM_ALPHA) * x_ref[...] + out)
    o_ref[...] = y * lng_ref[...] + lnb_ref[...]


def _finish(a_pre, b_pre, p, x2, woa, wob, wout, ln_g, ln_b):
    tokens = x2.shape[0]
    tb = OUT_ROWS
    row = lambda i: (i, 0)
    const = lambda i: (0, 0)
    wspec = pl.BlockSpec((D_MODEL, D_MODEL), const, pipeline_mode=pl.Buffered(1))
    vmem = 3 * D_MODEL * D_MODEL * 2 + 2 * (4 * tb * D_MODEL * 2 + 2 * tb * D_MODEL * 4) + 6 * tb * D_MODEL * 4
    return pl.pallas_call(
        _out_kernel,
        grid=(tokens // tb,),
        in_specs=[
            pl.BlockSpec((tb, D_MODEL), row),
            pl.BlockSpec((tb, D_MODEL), row),
            pl.BlockSpec((tb, D_MODEL), lambda i: (i, 6)),
            pl.BlockSpec((tb, D_MODEL), lambda i: (i, 7)),
            pl.BlockSpec((tb, D_MODEL), row),
            wspec, wspec, wspec,
            pl.BlockSpec((1, D_MODEL), const),
            pl.BlockSpec((1, D_MODEL), const),
        ],
        out_specs=pl.BlockSpec((tb, D_MODEL), row),
        out_shape=jax.ShapeDtypeStruct((tokens, D_MODEL), F32),
        compiler_params=pltpu.CompilerParams(
            dimension_semantics=("arbitrary",),
            vmem_limit_bytes=min(vmem + (6 << 20), V7X_VMEM_BYTES - (8 << 20))),
        name="finish",
    )(a_pre, b_pre, p, p, x2, woa, wob, wout, ln_g, ln_b)


def _rotary_tables(seq):
    half = R_QK_DIM // 2
    freqs = ROPE_BASE ** (-jnp.arange(0, R_QK_DIM, 2, dtype=F32) / R_QK_DIM)
    ang = jnp.arange(seq, dtype=jnp.int32).astype(F32)[:, None] * freqs[None, :]
    cos, sin = jnp.cos(ang), jnp.sin(ang)
    assert cos.shape == (seq, half)
    return jnp.concatenate([cos, cos], axis=-1), jnp.concatenate([-sin, sin], axis=-1)


def _retention_constants():
    log_gamma = np.log1p(-np.exp2(-5.0 - np.arange(R_HEADS, dtype=np.float64)))
    idx = np.arange(CHUNK, dtype=np.float64)
    rel = idx[:, None] - idx[None, :]
    decay = np.where(rel[None] >= 0, np.exp(log_gamma[:, None, None] * np.maximum(rel, 0.0)[None]), 0.0)
    w_state = np.exp(log_gamma[:, None] * (CHUNK - 1 - idx)[None, :])
    cross_decay = np.exp(log_gamma[:, None] * (idx + 1.0)[None, :])
    chunk_decay = tuple(float(np.float32(v)) for v in np.exp(log_gamma * CHUNK))
    return (jnp.asarray(decay, F32), jnp.asarray(w_state[:, :, None], F32),
            jnp.asarray(cross_decay[:, :, None], F32), chunk_decay)


def kernel(x, w_in, b_gate, ln_v_g, ln_v_b, w_s, b_s, w_oa, w_ob, w_out, ln_g, ln_b):
    batch, seq, d = x.shape
    assert d == D_MODEL and w_in.shape == (DEPTH, D_MODEL, IN_WIDTH)
    assert seq % PROJ_ROWS == 0 and seq % MIX_ROWS == 0 and (batch * seq) % OUT_ROWS == 0
    tokens = batch * seq
    x2 = x.reshape(tokens, D_MODEL)
    cos_t, sin_t = _rotary_tables(seq)
    decay, wst3, cd3, chunk_decay = _retention_constants()
    causal = jnp.tril(jnp.ones((CHUNK, CHUNK), dtype=w_s.dtype))
    for l in range(DEPTH):
        p = _project(x2.astype(BF16), w_in[l].astype(BF16), ln_v_g[l][None, :], ln_v_b[l][None, :],
                     b_gate[l].reshape(2, D_MODEL), cos_t, sin_t, seq)
        ws_m = (w_s[l] * causal[None]).astype(BF16)
        a_pre, b_pre = _mix(p, ws_m, b_s[l][:, :, None], decay, wst3, cd3, chunk_decay, batch, seq)
        x2 = _finish(a_pre, b_pre, p, x2, w_oa[l].astype(BF16), w_ob[l].astype(BF16),
                     w_out[l].astype(BF16), ln_g[l][None, :], ln_b[l][None, :])
    return x2.reshape(batch, seq, D_MODEL)
```

```python
import functools

import numpy as np
import jax
import jax.numpy as jnp
from jax import lax
from jax.experimental import pallas as pl
from jax.experimental.pallas import tpu as pltpu

D_MODEL = 2048
CHUNK = 128
A_GROUPS = 8
A_GROUP_DIM = D_MODEL // A_GROUPS
R_HEADS = 8
R_QK_DIM = D_MODEL // (2 * R_HEADS)
R_V_DIM = D_MODEL // R_HEADS
R_QK_WIDTH = R_HEADS * R_QK_DIM
ROPE_BASE = 10000.0
LN_EPS = 1e-5
DEPTH = 1
DEEPNORM_ALPHA = (2 * DEPTH) ** 0.25
N_COL_GROUPS = 8
IN_WIDTH = N_COL_GROUPS * D_MODEL

V7X_VMEM_BYTES = 64 * 1024 * 1024
VMEM_CEILING = V7X_VMEM_BYTES - (6 << 20)

PROJ_ROWS = 512
MIX_ROWS = 256
OUT_ROWS = 512

F32 = jnp.float32
BF16 = jnp.bfloat16


def _gelu_tanh(x):
    c = np.float32(np.sqrt(2.0 / np.pi))
    return 0.5 * x * (1.0 + jnp.tanh(c * (x + np.float32(0.044715) * (x * x * x))))


def _sigmoid(x):
    return 0.5 * (jnp.tanh(0.5 * x) + 1.0)


def _silu(x):
    return x * _sigmoid(x)


def _standardize_rows(x):
    mu = jnp.mean(x, axis=-1, keepdims=True)
    d = x - mu
    var = jnp.mean(d * d, axis=-1, keepdims=True)
    return d * lax.rsqrt(var + LN_EPS)


def _proj_kernel(x_ref, w_ref, lng_ref, lnb_ref, bg_ref, cos_ref, sin_ref, dk_ref, o_ref):
    j = pl.program_id(0)

    def mm():
        return jnp.dot(x_ref[...], w_ref[...], preferred_element_type=F32)

    @pl.when((j == 0))
    def _():
        o_ref[...] = _gelu_tanh(mm()).astype(BF16)

    @pl.when(j == 1)
    def _():
        g = _standardize_rows(_gelu_tanh(mm()))
        o_ref[...] = (g * lng_ref[...] + lnb_ref[...]).astype(BF16)

    @pl.when((j == 2) | (j == 5))
    def _():
        o_ref[...] = _silu(mm()).astype(BF16)

    @pl.when(j == 3)
    def _():
        acc = mm()
        cos = cos_ref[...]
        sin = sin_ref[...]
        for c in range(2 * R_HEADS):
            cols = slice(c * R_QK_DIM, (c + 1) * R_QK_DIM)
            blk = acc[:, cols]
            rot = blk * cos + pltpu.roll(blk, R_QK_DIM // 2, 1) * sin
            for rr in range(x_ref.shape[0] // CHUNK):
                rows = slice(rr * CHUNK, (rr + 1) * CHUNK)
                o_ref[rows, cols] = (rot[rows] * dk_ref[:, cols]).astype(BF16)

    @pl.when(j == 4)
    def _():
        o_ref[...] = mm().astype(BF16)

    @pl.when(j == 6)
    def _():
        o_ref[...] = _sigmoid(mm() + bg_ref[0:1, :]).astype(BF16)

    @pl.when(j == 7)
    def _():
        o_ref[...] = _sigmoid(mm() + bg_ref[1:2, :]).astype(BF16)


def _project(xb, wb, ln_v_g, ln_v_b, b_gate2, cos_t, sin_t, dk_t, seq):
    tokens = xb.shape[0]
    bm = PROJ_ROWS
    n_i = tokens // bm
    pos_blocks = seq // bm
    vmem = 2 * (bm * D_MODEL * 2 + D_MODEL * D_MODEL * 2 + bm * D_MODEL * 2) + 4 * bm * D_MODEL * 4
    const = lambda j, i: (0, 0)
    return pl.pallas_call(
        _proj_kernel,
        grid=(N_COL_GROUPS, n_i),
        in_specs=[
            pl.BlockSpec((bm, D_MODEL), lambda j, i: (i, 0)),
            pl.BlockSpec((D_MODEL, D_MODEL), lambda j, i: (0, j)),
            pl.BlockSpec((1, D_MODEL), const),
            pl.BlockSpec((1, D_MODEL), const),
            pl.BlockSpec((2, D_MODEL), const),
            pl.BlockSpec((bm, R_QK_DIM), lambda j, i: (i % pos_blocks, 0)),
            pl.BlockSpec((bm, R_QK_DIM), lambda j, i: (i % pos_blocks, 0)),
            pl.BlockSpec((CHUNK, D_MODEL), const),
        ],
        out_specs=pl.BlockSpec((bm, D_MODEL), lambda j, i: (i, j)),
        out_shape=jax.ShapeDtypeStruct((tokens, IN_WIDTH), BF16),
        compiler_params=pltpu.CompilerParams(
            dimension_semantics=("arbitrary", "arbitrary"),
            vmem_limit_bytes=min(vmem + (8 << 20), VMEM_CEILING)),
        name="proj",
    )(xb, wb, ln_v_g, ln_v_b, b_gate2, cos_t, sin_t, dk_t)


def _mix_rows(chunk_decay, u_ref, vn_ref, az_ref, qk_ref, v_ref, rz_ref, ws_ref, bsb_ref,
              state, a_scr, b_scr):
    rows = u_ref.shape[0]
    t_idx = lax.broadcasted_iota(jnp.int32, (CHUNK, CHUNK), 0)
    s_idx = lax.broadcasted_iota(jnp.int32, (CHUNK, CHUNK), 1)
    causal = t_idx >= s_idx
    for c in range(rows // CHUNK):
        r = slice(c * CHUNK, (c + 1) * CHUNK)
        for g in range(A_GROUPS):
            cs = slice(g * A_GROUP_DIM, (g + 1) * A_GROUP_DIM)
            ws_g = jnp.where(causal, ws_ref[g], 0.0).astype(BF16)
            sv = jnp.dot(ws_g, vn_ref[r, cs], preferred_element_type=F32)
            bias = bsb_ref[g]
            sv = jnp.concatenate([sv[:, :CHUNK] + bias, sv[:, CHUNK:] + bias], axis=1)
            a = u_ref[r, cs].astype(F32) * sv * az_ref[r, cs].astype(F32)
            a_scr[r, cs] = a.astype(BF16)
        for h in range(R_HEADS):
            q = qk_ref[r, h * R_QK_DIM:(h + 1) * R_QK_DIM]
            k = qk_ref[r, R_QK_WIDTH + h * R_QK_DIM:R_QK_WIDTH + (h + 1) * R_QK_DIM]
            vs = slice(h * R_V_DIM, (h + 1) * R_V_DIM)
            v = v_ref[r, vs]
            s = lax.dot_general(q, k, (((1,), (1,)), ((), ())), preferred_element_type=F32)
            s = jnp.where(causal, s, 0.0).astype(BF16)
            st = state[h]
            lhs = jnp.concatenate([s, q], axis=1)
            rhs = jnp.concatenate([v, st.astype(BF16)], axis=0)
            o = _standardize_rows(jnp.dot(lhs, rhs, preferred_element_type=F32))
            b_scr[r, vs] = (o * rz_ref[r, vs].astype(F32)).astype(BF16)
            kv = lax.dot_general(k, v, (((0,), (0,)), ((), ())), preferred_element_type=F32)
            state[h] = (st + kv) * chunk_decay[h]


def _branch_proj(a_scr, b_scr, ga_ref, gb_ref, woa_ref, wob_ref, m_ref):
    ya = jnp.dot(a_scr[...], woa_ref[...], preferred_element_type=F32)
    yb = jnp.dot(b_scr[...], wob_ref[...], preferred_element_type=F32)
    m_ref[...] = (ga_ref[...].astype(F32) * ya + gb_ref[...].astype(F32) * yb).astype(BF16)


def _mixproj_kernel(chunk_decay, blocks_per_seq,
                    u_ref, vn_ref, az_ref, qk_ref, v_ref, rz_ref, ga_ref, gb_ref,
                    ws_ref, bsb_ref, woa_ref, wob_ref, m_ref,
                    state, a0, b0, a1, b1):
    t = pl.program_id(0)
    n_blocks = pl.num_programs(0) - 1

    @pl.when(t % blocks_per_seq == 0)
    def _():
        state[...] = jnp.zeros_like(state)

    mix = functools.partial(_mix_rows, chunk_decay, u_ref, vn_ref, az_ref, qk_ref, v_ref, rz_ref,
                            ws_ref, bsb_ref, state)
    proj = lambda a, b: _branch_proj(a, b, ga_ref, gb_ref, woa_ref, wob_ref, m_ref)

    @pl.when(t == 0)
    def _():
        mix(a0, b0)

    @pl.when((t > 0) & (t < n_blocks) & (t % 2 == 1))
    def _():
        mix(a1, b1)
        proj(a0, b0)

    @pl.when((t > 0) & (t < n_blocks) & (t % 2 == 0))
    def _():
        mix(a0, b0)
        proj(a1, b1)

    @pl.when(t == n_blocks)
    def _():
        proj(a1, b1)


def _mixproj(p, ws, bsb, woa, wob, chunk_decay, seq):
    tokens = p.shape[0]
    tb = MIX_ROWS
    n_blocks = tokens // tb
    assert n_blocks % 2 == 0
    last = n_blocks - 1

    def cur(jcol):
        return pl.BlockSpec((tb, D_MODEL), lambda t: (jnp.minimum(t, last), jcol))

    def prev(jcol):
        return pl.BlockSpec((tb, D_MODEL), lambda t: (jnp.maximum(t - 1, 0), jcol))

    def whole(a):
        nd = a.ndim
        return pl.BlockSpec(a.shape, lambda t: (0,) * nd)

    wspec = pl.BlockSpec((D_MODEL, D_MODEL), lambda t: (0, 0), pipeline_mode=pl.Buffered(1))
    blk = tb * D_MODEL
    vmem = (2 * D_MODEL * D_MODEL * 2 + 2 * 8 * blk * 2 + 2 * blk * 2 + 4 * blk * 2
            + R_HEADS * R_QK_DIM * R_V_DIM * 4 + 4 * blk * 4 + (4 << 20))
    return pl.pallas_call(
        functools.partial(_mixproj_kernel, chunk_decay, seq // tb),
        grid=(n_blocks + 1,),
        in_specs=[cur(0), cur(1), cur(2), cur(3), cur(4), cur(5), prev(6), prev(7),
                  whole(ws), whole(bsb), wspec, wspec],
        out_specs=pl.BlockSpec((tb, D_MODEL), lambda t: (jnp.maximum(t - 1, 0), 0)),
        out_shape=jax.ShapeDtypeStruct((tokens, D_MODEL), BF16),
        scratch_shapes=[pltpu.VMEM((R_HEADS, R_QK_DIM, R_V_DIM), F32)]
        + [pltpu.VMEM((tb, D_MODEL), BF16)] * 4,
        compiler_params=pltpu.CompilerParams(
            dimension_semantics=("arbitrary",),
            vmem_limit_bytes=min(vmem + (6 << 20), VMEM_CEILING)),
        name="mixproj",
    )(p, p, p, p, p, p, p, p, ws, bsb, woa, wob)


def _out_kernel(m_ref, x_ref, wout_ref, lng_ref, lnb_ref, o_ref):
    out = jnp.dot(m_ref[...], wout_ref[...], preferred_element_type=F32)
    y = _standardize_rows(np.float32(DEEPNORM_ALPHA) * x_ref[...] + out)
    o_ref[...] = y * lng_ref[...] + lnb_ref[...]


def _finish(m, x2, wout, ln_g, ln_b):
    tokens = x2.shape[0]
    tb = OUT_ROWS
    row = lambda i: (i, 0)
    const = lambda i: (0, 0)
    blk = tb * D_MODEL
    vmem = D_MODEL * D_MODEL * 2 + 2 * (blk * 2 + 2 * blk * 4) + 4 * blk * 4
    return pl.pallas_call(
        _out_kernel,
        grid=(tokens // tb,),
        in_specs=[
            pl.BlockSpec((tb, D_MODEL), row),
            pl.BlockSpec((tb, D_MODEL), row),
            pl.BlockSpec((D_MODEL, D_MODEL), const, pipeline_mode=pl.Buffered(1)),
            pl.BlockSpec((1, D_MODEL), const),
            pl.BlockSpec((1, D_MODEL), const),
        ],
        out_specs=pl.BlockSpec((tb, D_MODEL), row),
        out_shape=jax.ShapeDtypeStruct((tokens, D_MODEL), F32),
        compiler_params=pltpu.CompilerParams(
            dimension_semantics=("arbitrary",),
            vmem_limit_bytes=min(vmem + (6 << 20), VMEM_CEILING)),
        name="finish",
    )(m, x2, wout, ln_g, ln_b)


def _rotary_tables(seq):
    half = R_QK_DIM // 2
    freqs = ROPE_BASE ** (-jnp.arange(0, R_QK_DIM, 2, dtype=F32) / R_QK_DIM)
    ang = jnp.arange(seq, dtype=jnp.int32).astype(F32)[:, None] * freqs[None, :]
    cos, sin = jnp.cos(ang), jnp.sin(ang)
    assert cos.shape == (seq, half)
    return jnp.concatenate([cos, cos], axis=-1), jnp.concatenate([-sin, sin], axis=-1)


def _retention_constants():
    log_gamma = np.log1p(-np.exp2(-5.0 - np.arange(R_HEADS, dtype=np.float64)))
    pos = np.arange(CHUNK, dtype=np.float64) + 1.0
    q_fac = np.exp(log_gamma[None, :] * pos[:, None])
    k_fac = np.exp(-log_gamma[None, :] * pos[:, None]) * R_QK_DIM ** -0.5
    table = np.concatenate([np.repeat(q_fac, R_QK_DIM, axis=1), np.repeat(k_fac, R_QK_DIM, axis=1)], axis=1)
    chunk_decay = tuple(float(np.float32(v)) for v in np.exp(log_gamma * CHUNK))
    return jnp.asarray(table, F32), chunk_decay


def kernel(x, w_in, b_gate, ln_v_g, ln_v_b, w_s, b_s, w_oa, w_ob, w_out, ln_g, ln_b):
    batch, seq, d = x.shape
    assert d == D_MODEL and w_in.shape == (DEPTH, D_MODEL, IN_WIDTH)
    assert seq % PROJ_ROWS == 0 and seq % MIX_ROWS == 0 and (batch * seq) % OUT_ROWS == 0
    tokens = batch * seq
    x2 = x.reshape(tokens, D_MODEL)
    cos_t, sin_t = _rotary_tables(seq)
    dk_t, chunk_decay = _retention_constants()
    for l in range(DEPTH):
        p = _project(x2.astype(BF16), w_in[l].astype(BF16), ln_v_g[l][None, :], ln_v_b[l][None, :],
                     b_gate[l].reshape(2, D_MODEL), cos_t, sin_t, dk_t, seq)
        bsb = jnp.broadcast_to(b_s[l][:, :, None], (A_GROUPS, CHUNK, CHUNK))
        m = _mixproj(p, w_s[l], bsb, w_oa[l].astype(BF16), w_ob[l].astype(BF16), chunk_decay, seq)
        x2 = _finish(m, x2, w_out[l].astype(BF16), ln_g[l][None, :], ln_b[l][None, :])
    return x2.reshape(batch, seq, D_MODEL)
```

```python
import functools

import numpy as np
import jax
import jax.numpy as jnp
from jax import lax
from jax.experimental import pallas as pl
from jax.experimental.pallas import tpu as pltpu

D_MODEL = 2048
CHUNK = 128
A_GROUPS = 8
A_GROUP_DIM = D_MODEL // A_GROUPS
R_HEADS = 8
R_QK_DIM = D_MODEL // (2 * R_HEADS)
R_V_DIM = D_MODEL // R_HEADS
R_QK_WIDTH = R_HEADS * R_QK_DIM
ROPE_BASE = 10000.0
LN_EPS = 1e-5
DEPTH = 1
DEEPNORM_ALPHA = (2 * DEPTH) ** 0.25
N_COL_GROUPS = 8
IN_WIDTH = N_COL_GROUPS * D_MODEL

V7X_VMEM_BYTES = 64 * 1024 * 1024
VMEM_CEILING = V7X_VMEM_BYTES - (6 << 20)

PROJ_ROWS = 512
MIX_ROWS = 256
OUT_ROWS = 512

F32 = jnp.float32
BF16 = jnp.bfloat16


def _gelu_tanh(x):
    c = np.float32(np.sqrt(2.0 / np.pi))
    return 0.5 * x * (1.0 + jnp.tanh(c * (x + np.float32(0.044715) * (x * x * x))))


def _sigmoid(x):
    return 0.5 * (jnp.tanh(0.5 * x) + 1.0)


def _silu(x):
    return x * _sigmoid(x)


def _standardize_rows(x):
    mu = jnp.mean(x, axis=-1, keepdims=True)
    d = x - mu
    var = jnp.mean(d * d, axis=-1, keepdims=True)
    return d * lax.rsqrt(var + LN_EPS)


def _proj_kernel(x_ref, w_ref, lng_ref, lnb_ref, bg_ref, cos_ref, sin_ref, dk_ref, o_ref):
    j = pl.program_id(0)

    def mm():
        return jnp.dot(x_ref[...], w_ref[...], preferred_element_type=F32)

    @pl.when((j == 0))
    def _():
        o_ref[...] = _gelu_tanh(mm()).astype(BF16)

    @pl.when(j == 1)
    def _():
        g = _standardize_rows(_gelu_tanh(mm()))
        o_ref[...] = (g * lng_ref[...] + lnb_ref[...]).astype(BF16)

    @pl.when((j == 2) | (j == 5))
    def _():
        o_ref[...] = _silu(mm()).astype(BF16)

    @pl.when(j == 3)
    def _():
        acc = mm()
        cos = cos_ref[...]
        sin = sin_ref[...]
        for c in range(2 * R_HEADS):
            cols = slice(c * R_QK_DIM, (c + 1) * R_QK_DIM)
            blk = acc[:, cols]
            rot = blk * cos + pltpu.roll(blk, R_QK_DIM // 2, 1) * sin
            for rr in range(x_ref.shape[0] // CHUNK):
                rows = slice(rr * CHUNK, (rr + 1) * CHUNK)
                o_ref[rows, cols] = (rot[rows] * dk_ref[:, cols]).astype(BF16)

    @pl.when(j == 4)
    def _():
        o_ref[...] = mm().astype(BF16)

    @pl.when(j == 6)
    def _():
        o_ref[...] = _sigmoid(mm() + bg_ref[0:1, :]).astype(BF16)

    @pl.when(j == 7)
    def _():
        o_ref[...] = _sigmoid(mm() + bg_ref[1:2, :]).astype(BF16)


def _project(xb, wb, ln_v_g, ln_v_b, b_gate2, cos_t, sin_t, dk_t, seq):
    tokens = xb.shape[0]
    bm = PROJ_ROWS
    n_i = tokens // bm
    pos_blocks = seq // bm
    vmem = 2 * (bm * D_MODEL * 2 + D_MODEL * D_MODEL * 2 + bm * D_MODEL * 2) + 4 * bm * D_MODEL * 4
    const = lambda j, i: (0, 0)
    return pl.pallas_call(
        _proj_kernel,
        grid=(N_COL_GROUPS, n_i),
        in_specs=[
            pl.BlockSpec((bm, D_MODEL), lambda j, i: (i, 0)),
            pl.BlockSpec((D_MODEL, D_MODEL), lambda j, i: (0, j)),
            pl.BlockSpec((1, D_MODEL), const),
            pl.BlockSpec((1, D_MODEL), const),
            pl.BlockSpec((2, D_MODEL), const),
            pl.BlockSpec((bm, R_QK_DIM), lambda j, i: (i % pos_blocks, 0)),
            pl.BlockSpec((bm, R_QK_DIM), lambda j, i: (i % pos_blocks, 0)),
            pl.BlockSpec((CHUNK, D_MODEL), const),
        ],
        out_specs=pl.BlockSpec((bm, D_MODEL), lambda j, i: (i, j)),
        out_shape=jax.ShapeDtypeStruct((tokens, IN_WIDTH), BF16),
        compiler_params=pltpu.CompilerParams(
            dimension_semantics=("arbitrary", "arbitrary"),
            vmem_limit_bytes=min(vmem + (8 << 20), VMEM_CEILING)),
        name="proj",
    )(xb, wb, ln_v_g, ln_v_b, b_gate2, cos_t, sin_t, dk_t)


def _mix_and_proj(chunk_decay, u_ref, vn_ref, az_ref, qk_ref, v_ref, rz_ref, ga_ref, gb_ref,
                  ws_ref, bsb_ref, woa_ref, wob_ref, m_ref, state, mix_to, proj_from):
    rows = u_ref.shape[0]
    heads = [(c, h) for c in range(rows // CHUNK) for h in range(R_HEADS)]
    n_tiles = D_MODEL // R_V_DIM
    assert len(heads) == 2 * n_tiles
    if mix_to is not None:
        a_scr, b_scr = mix_to
        t_idx = lax.broadcasted_iota(jnp.int32, (CHUNK, CHUNK), 0)
        s_idx = lax.broadcasted_iota(jnp.int32, (CHUNK, CHUNK), 1)
        causal = t_idx >= s_idx

    def qk(c, h):
        r = slice(c * CHUNK, (c + 1) * CHUNK)
        q = qk_ref[r, h * R_QK_DIM:(h + 1) * R_QK_DIM]
        k = qk_ref[r, R_QK_WIDTH + h * R_QK_DIM:R_QK_WIDTH + (h + 1) * R_QK_DIM]
        return lax.dot_general(q, k, (((1,), (1,)), ((), ())), preferred_element_type=F32)

    def spatial_gate(c, g):
        r = slice(c * CHUNK, (c + 1) * CHUNK)
        cs = slice(g * A_GROUP_DIM, (g + 1) * A_GROUP_DIM)
        ws_g = jnp.where(causal, ws_ref[g], 0.0).astype(BF16)
        sv = jnp.dot(ws_g, vn_ref[r, cs], preferred_element_type=F32)
        bias = bsb_ref[g]
        sv = jnp.concatenate([sv[:, :CHUNK] + bias, sv[:, CHUNK:] + bias], axis=1)
        a = u_ref[r, cs].astype(F32) * sv * az_ref[r, cs].astype(F32)
        a_scr[r, cs] = a.astype(BF16)

    def retention(c, h, s):
        r = slice(c * CHUNK, (c + 1) * CHUNK)
        q = qk_ref[r, h * R_QK_DIM:(h + 1) * R_QK_DIM]
        k = qk_ref[r, R_QK_WIDTH + h * R_QK_DIM:R_QK_WIDTH + (h + 1) * R_QK_DIM]
        vs = slice(h * R_V_DIM, (h + 1) * R_V_DIM)
        v = v_ref[r, vs]
        s = jnp.where(causal, s, 0.0).astype(BF16)
        st = state[h]
        lhs = jnp.concatenate([s, q], axis=1)
        rhs = jnp.concatenate([v, st.astype(BF16)], axis=0)
        o = _standardize_rows(jnp.dot(lhs, rhs, preferred_element_type=F32))
        b_scr[r, vs] = (o * rz_ref[r, vs].astype(F32)).astype(BF16)
        kv = lax.dot_general(k, v, (((0,), (0,)), ((), ())), preferred_element_type=F32)
        state[h] = (st + kv) * chunk_decay[h]

    ya = None
    scores = qk(*heads[0]) if mix_to is not None else None
    for idx, (c, h) in enumerate(heads):
        if mix_to is not None:
            nxt = qk(*heads[idx + 1]) if idx + 1 < len(heads) else None
            spatial_gate(c, h)
        if proj_from is not None:
            cols = slice((idx // 2) * R_V_DIM, (idx // 2 + 1) * R_V_DIM)
            if idx % 2 == 0:
                ya = jnp.dot(proj_from[0][...], woa_ref[:, cols], preferred_element_type=F32)
            else:
                yb = jnp.dot(proj_from[1][...], wob_ref[:, cols], preferred_element_type=F32)
                merged = ga_ref[:, cols].astype(F32) * ya + gb_ref[:, cols].astype(F32) * yb
                m_ref[:, cols] = merged.astype(BF16)
        if mix_to is not None:
            retention(c, h, scores)
            scores = nxt


def _mixproj_kernel(chunk_decay, blocks_per_seq,
                    u_ref, vn_ref, az_ref, qk_ref, v_ref, rz_ref, ga_ref, gb_ref,
                    ws_ref, bsb_ref, woa_ref, wob_ref, m_ref,
                    state, a0, b0, a1, b1):
    t = pl.program_id(0)
    n_blocks = pl.num_programs(0) - 1

    @pl.when(t % blocks_per_seq == 0)
    def _():
        state[...] = jnp.zeros_like(state)

    step = functools.partial(_mix_and_proj, chunk_decay, u_ref, vn_ref, az_ref, qk_ref, v_ref, rz_ref,
                             ga_ref, gb_ref, ws_ref, bsb_ref, woa_ref, wob_ref, m_ref, state)

    @pl.when(t == 0)
    def _():
        step((a0, b0), None)

    @pl.when((t > 0) & (t < n_blocks) & (t % 2 == 1))
    def _():
        step((a1, b1), (a0, b0))

    @pl.when((t > 0) & (t < n_blocks) & (t % 2 == 0))
    def _():
        step((a0, b0), (a1, b1))

    @pl.when(t == n_blocks)
    def _():
        step(None, (a1, b1))


def _mixproj(p, ws, bsb, woa, wob, chunk_decay, seq):
    tokens = p.shape[0]
    tb = MIX_ROWS
    n_blocks = tokens // tb
    assert n_blocks % 2 == 0
    last = n_blocks - 1

    def cur(jcol):
        return pl.BlockSpec((tb, D_MODEL), lambda t: (jnp.minimum(t, last), jcol))

    def prev(jcol):
        return pl.BlockSpec((tb, D_MODEL), lambda t: (jnp.maximum(t - 1, 0), jcol))

    def whole(a):
        nd = a.ndim
        return pl.BlockSpec(a.shape, lambda t: (0,) * nd)

    wspec = pl.BlockSpec((D_MODEL, D_MODEL), lambda t: (0, 0), pipeline_mode=pl.Buffered(1))
    blk = tb * D_MODEL
    vmem = (2 * D_MODEL * D_MODEL * 2 + 2 * 8 * blk * 2 + 2 * blk * 2 + 4 * blk * 2
            + R_HEADS * R_QK_DIM * R_V_DIM * 4 + 4 * blk * 4 + (4 << 20))
    return pl.pallas_call(
        functools.partial(_mixproj_kernel, chunk_decay, seq // tb),
        grid=(n_blocks + 1,),
        in_specs=[cur(0), cur(1), cur(2), cur(3), cur(4), cur(5), prev(6), prev(7),
                  whole(ws), whole(bsb), wspec, wspec],
        out_specs=pl.BlockSpec((tb, D_MODEL), lambda t: (jnp.maximum(t - 1, 0), 0)),
        out_shape=jax.ShapeDtypeStruct((tokens, D_MODEL), BF16),
        scratch_shapes=[pltpu.VMEM((R_HEADS, R_QK_DIM, R_V_DIM), F32)]
        + [pltpu.VMEM((tb, D_MODEL), BF16)] * 4,
        compiler_params=pltpu.CompilerParams(
            dimension_semantics=("arbitrary",),
            vmem_limit_bytes=min(vmem + (6 << 20), VMEM_CEILING)),
        name="mixproj",
    )(p, p, p, p, p, p, p, p, ws, bsb, woa, wob)


def _out_kernel(m_ref, x_ref, wout_ref, lng_ref, lnb_ref, o_ref):
    out = jnp.dot(m_ref[...], wout_ref[...], preferred_element_type=F32)
    y = _standardize_rows(np.float32(DEEPNORM_ALPHA) * x_ref[...] + out)
    o_ref[...] = y * lng_ref[...] + lnb_ref[...]


def _finish(m, x2, wout, ln_g, ln_b):
    tokens = x2.shape[0]
    tb = OUT_ROWS
    row = lambda i: (i, 0)
    const = lambda i: (0, 0)
    blk = tb * D_MODEL
    vmem = D_MODEL * D_MODEL * 2 + 2 * (blk * 2 + 2 * blk * 4) + 4 * blk * 4
    return pl.pallas_call(
        _out_kernel,
        grid=(tokens // tb,),
        in_specs=[
            pl.BlockSpec((tb, D_MODEL), row),
            pl.BlockSpec((tb, D_MODEL), row),
            pl.BlockSpec((D_MODEL, D_MODEL), const, pipeline_mode=pl.Buffered(1)),
            pl.BlockSpec((1, D_MODEL), const),
            pl.BlockSpec((1, D_MODEL), const),
        ],
        out_specs=pl.BlockSpec((tb, D_MODEL), row),
        out_shape=jax.ShapeDtypeStruct((tokens, D_MODEL), F32),
        compiler_params=pltpu.CompilerParams(
            dimension_semantics=("arbitrary",),
            vmem_limit_bytes=min(vmem + (6 << 20), VMEM_CEILING)),
        name="finish",
    )(m, x2, wout, ln_g, ln_b)


def _rotary_tables(seq):
    half = R_QK_DIM // 2
    freqs = ROPE_BASE ** (-jnp.arange(0, R_QK_DIM, 2, dtype=F32) / R_QK_DIM)
    ang = jnp.arange(seq, dtype=jnp.int32).astype(F32)[:, None] * freqs[None, :]
    cos, sin = jnp.cos(ang), jnp.sin(ang)
    assert cos.shape == (seq, half)
    return jnp.concatenate([cos, cos], axis=-1), jnp.concatenate([-sin, sin], axis=-1)


def _retention_constants():
    log_gamma = np.log1p(-np.exp2(-5.0 - np.arange(R_HEADS, dtype=np.float64)))
    pos = np.arange(CHUNK, dtype=np.float64) + 1.0
    q_fac = np.exp(log_gamma[None, :] * pos[:, None])
    k_fac = np.exp(-log_gamma[None, :] * pos[:, None]) * R_QK_DIM ** -0.5
    table = np.concatenate([np.repeat(q_fac, R_QK_DIM, axis=1), np.repeat(k_fac, R_QK_DIM, axis=1)], axis=1)
    chunk_decay = tuple(float(np.float32(v)) for v in np.exp(log_gamma * CHUNK))
    return jnp.asarray(table, F32), chunk_decay


def kernel(x, w_in, b_gate, ln_v_g, ln_v_b, w_s, b_s, w_oa, w_ob, w_out, ln_g, ln_b):
    batch, seq, d = x.shape
    assert d == D_MODEL and w_in.shape == (DEPTH, D_MODEL, IN_WIDTH)
    assert seq % PROJ_ROWS == 0 and seq % MIX_ROWS == 0 and (batch * seq) % OUT_ROWS == 0
    tokens = batch * seq
    x2 = x.reshape(tokens, D_MODEL)
    cos_t, sin_t = _rotary_tables(seq)
    dk_t, chunk_decay = _retention_constants()
    for l in range(DEPTH):
        p = _project(x2.astype(BF16), w_in[l].astype(BF16), ln_v_g[l][None, :], ln_v_b[l][None, :],
                     b_gate[l].reshape(2, D_MODEL), cos_t, sin_t, dk_t, seq)
        bsb = jnp.broadcast_to(b_s[l][:, :, None], (A_GROUPS, CHUNK, CHUNK))
        m = _mixproj(p, w_s[l], bsb, w_oa[l].astype(BF16), w_ob[l].astype(BF16), chunk_decay, seq)
        x2 = _finish(m, x2, w_out[l].astype(BF16), ln_g[l][None, :], ln_b[l][None, :])
    return x2.reshape(batch, seq, D_MODEL)
```

```python
import functools

import numpy as np
import jax
import jax.numpy as jnp
from jax import lax
from jax.experimental import pallas as pl
from jax.experimental.pallas import tpu as pltpu

D_MODEL = 2048
CHUNK = 128
A_GROUPS = 8
A_GROUP_DIM = D_MODEL // A_GROUPS
R_HEADS = 8
R_QK_DIM = D_MODEL // (2 * R_HEADS)
R_V_DIM = D_MODEL // R_HEADS
R_QK_WIDTH = R_HEADS * R_QK_DIM
ROPE_BASE = 10000.0
LN_EPS = 1e-5
DEPTH = 1
DEEPNORM_ALPHA = (2 * DEPTH) ** 0.25
N_COL_GROUPS = 8
IN_WIDTH = N_COL_GROUPS * D_MODEL

V7X_VMEM_BYTES = 64 * 1024 * 1024
VMEM_CEILING = V7X_VMEM_BYTES - (6 << 20)

PROJ_ROWS = 1024
MIX_ROWS = 256
OUT_ROWS = 1024
SLAB_ROWS = 256

F32 = jnp.float32
BF16 = jnp.bfloat16


def _gelu_tanh(x):
    c = np.float32(np.sqrt(2.0 / np.pi))
    return 0.5 * x * (1.0 + jnp.tanh(c * (x + np.float32(0.044715) * (x * x * x))))


def _sigmoid(x):
    return 0.5 * (jnp.tanh(0.5 * x) + 1.0)


def _silu(x):
    return x * _sigmoid(x)


def _standardize_rows(x):
    mu = jnp.mean(x, axis=-1, keepdims=True)
    d = x - mu
    var = jnp.mean(d * d, axis=-1, keepdims=True)
    return d * lax.rsqrt(var + LN_EPS)


def _proj_kernel(x_ref, w_ref, lng_ref, lnb_ref, bg_ref, cos_ref, sin_ref, dk_ref, o_ref):
    j = pl.program_id(0)

    def per_slab(epilogue):
        for q in range(x_ref.shape[0] // SLAB_ROWS):
            rows = slice(q * SLAB_ROWS, (q + 1) * SLAB_ROWS)
            acc = jnp.dot(x_ref[rows, :].astype(BF16), w_ref[...], preferred_element_type=F32)
            epilogue(acc, rows)

    def store(fn):
        def epilogue(acc, rows):
            o_ref[rows, :] = fn(acc).astype(BF16)
        return epilogue

    @pl.when((j == 0))
    def _():
        per_slab(store(_gelu_tanh))

    @pl.when(j == 1)
    def _():
        per_slab(store(lambda acc: _standardize_rows(_gelu_tanh(acc)) * lng_ref[...] + lnb_ref[...]))

    @pl.when((j == 2) | (j == 5))
    def _():
        per_slab(store(_silu))

    @pl.when(j == 3)
    def _():
        def epilogue(acc, rows):
            cos = cos_ref[rows, :]
            sin = sin_ref[rows, :]
            for c in range(2 * R_HEADS):
                cols = slice(c * R_QK_DIM, (c + 1) * R_QK_DIM)
                blk = acc[:, cols]
                rot = blk * cos + pltpu.roll(blk, R_QK_DIM // 2, 1) * sin
                for rr in range(SLAB_ROWS // CHUNK):
                    sub = slice(rr * CHUNK, (rr + 1) * CHUNK)
                    dst = slice(rows.start + rr * CHUNK, rows.start + (rr + 1) * CHUNK)
                    o_ref[dst, cols] = (rot[sub] * dk_ref[:, cols]).astype(BF16)
        per_slab(epilogue)

    @pl.when(j == 4)
    def _():
        per_slab(store(lambda acc: acc))

    @pl.when(j == 6)
    def _():
        per_slab(store(lambda acc: _sigmoid(acc + bg_ref[0:1, :])))

    @pl.when(j == 7)
    def _():
        per_slab(store(lambda acc: _sigmoid(acc + bg_ref[1:2, :])))


def _project(xb, wb, ln_v_g, ln_v_b, b_gate2, cos_t, sin_t, dk_t, seq):
    tokens = xb.shape[0]
    bm = PROJ_ROWS
    n_i = tokens // bm
    pos_blocks = seq // bm
    vmem = (2 * (bm * D_MODEL * xb.dtype.itemsize + D_MODEL * D_MODEL * 2 + bm * D_MODEL * 2)
            + 4 * SLAB_ROWS * D_MODEL * 4)
    const = lambda j, i: (0, 0)
    return pl.pallas_call(
        _proj_kernel,
        grid=(N_COL_GROUPS, n_i),
        in_specs=[
            pl.BlockSpec((bm, D_MODEL), lambda j, i: (i, 0)),
            pl.BlockSpec((D_MODEL, D_MODEL), lambda j, i: (0, j)),
            pl.BlockSpec((1, D_MODEL), const),
            pl.BlockSpec((1, D_MODEL), const),
            pl.BlockSpec((2, D_MODEL), const),
            pl.BlockSpec((bm, R_QK_DIM), lambda j, i: (i % pos_blocks, 0)),
            pl.BlockSpec((bm, R_QK_DIM), lambda j, i: (i % pos_blocks, 0)),
            pl.BlockSpec((CHUNK, D_MODEL), const),
        ],
        out_specs=pl.BlockSpec((bm, D_MODEL), lambda j, i: (i, j)),
        out_shape=jax.ShapeDtypeStruct((tokens, IN_WIDTH), BF16),
        compiler_params=pltpu.CompilerParams(
            dimension_semantics=("arbitrary", "arbitrary"),
            vmem_limit_bytes=min(vmem + (8 << 20), VMEM_CEILING)),
        name="proj",
    )(xb, wb, ln_v_g, ln_v_b, b_gate2, cos_t, sin_t, dk_t)


def _mix_and_proj(chunk_decay, u_ref, vn_ref, az_ref, qk_ref, v_ref, rz_ref, ga_ref, gb_ref,
                  ws_ref, bsb_ref, woa_ref, wob_ref, m_ref, state, mix_to, proj_from):
    rows = u_ref.shape[0]
    heads = [(c, h) for c in range(rows // CHUNK) for h in range(R_HEADS)]
    n_tiles = D_MODEL // R_V_DIM
    assert len(heads) == 2 * n_tiles
    if mix_to is not None:
        a_scr, b_scr = mix_to
        t_idx = lax.broadcasted_iota(jnp.int32, (CHUNK, CHUNK), 0)
        s_idx = lax.broadcasted_iota(jnp.int32, (CHUNK, CHUNK), 1)
        causal = t_idx >= s_idx

    def qk(c, h):
        r = slice(c * CHUNK, (c + 1) * CHUNK)
        q = qk_ref[r, h * R_QK_DIM:(h + 1) * R_QK_DIM]
        k = qk_ref[r, R_QK_WIDTH + h * R_QK_DIM:R_QK_WIDTH + (h + 1) * R_QK_DIM]
        return lax.dot_general(q, k, (((1,), (1,)), ((), ())), preferred_element_type=F32)

    def spatial_gate(c, g):
        r = slice(c * CHUNK, (c + 1) * CHUNK)
        cs = slice(g * A_GROUP_DIM, (g + 1) * A_GROUP_DIM)
        ws_g = jnp.where(causal, ws_ref[g], 0.0).astype(BF16)
        sv = jnp.dot(ws_g, vn_ref[r, cs], preferred_element_type=F32)
        bias = bsb_ref[g]
        sv = jnp.concatenate([sv[:, :CHUNK] + bias, sv[:, CHUNK:] + bias], axis=1)
        a = u_ref[r, cs].astype(F32) * sv * az_ref[r, cs].astype(F32)
        a_scr[r, cs] = a.astype(BF16)

    def retention(c, h, s):
        r = slice(c * CHUNK, (c + 1) * CHUNK)
        q = qk_ref[r, h * R_QK_DIM:(h + 1) * R_QK_DIM]
        k = qk_ref[r, R_QK_WIDTH + h * R_QK_DIM:R_QK_WIDTH + (h + 1) * R_QK_DIM]
        vs = slice(h * R_V_DIM, (h + 1) * R_V_DIM)
        v = v_ref[r, vs]
        s = jnp.where(causal, s, 0.0).astype(BF16)
        st = state[h]
        lhs = jnp.concatenate([s, q], axis=1)
        rhs = jnp.concatenate([v, st.astype(BF16)], axis=0)
        o = _standardize_rows(jnp.dot(lhs, rhs, preferred_element_type=F32))
        b_scr[r, vs] = (o * rz_ref[r, vs].astype(F32)).astype(BF16)
        kv = lax.dot_general(k, v, (((0,), (0,)), ((), ())), preferred_element_type=F32)
        state[h] = (st + kv) * chunk_decay[h]

    ya = None
    scores = qk(*heads[0]) if mix_to is not None else None
    for idx, (c, h) in enumerate(heads):
        if mix_to is not None:
            nxt = qk(*heads[idx + 1]) if idx + 1 < len(heads) else None
            spatial_gate(c, h)
        if proj_from is not None:
            cols = slice((idx // 2) * R_V_DIM, (idx // 2 + 1) * R_V_DIM)
            if idx % 2 == 0:
                ya = jnp.dot(proj_from[0][...], woa_ref[:, cols], preferred_element_type=F32)
            else:
                yb = jnp.dot(proj_from[1][...], wob_ref[:, cols], preferred_element_type=F32)
                merged = ga_ref[:, cols].astype(F32) * ya + gb_ref[:, cols].astype(F32) * yb
                m_ref[:, cols] = merged.astype(BF16)
        if mix_to is not None:
            retention(c, h, scores)
            scores = nxt


def _mixproj_kernel(chunk_decay, blocks_per_seq,
                    u_ref, vn_ref, az_ref, qk_ref, v_ref, rz_ref, ga_ref, gb_ref,
                    ws_ref, bsb_ref, woa_ref, wob_ref, m_ref,
                    state, a0, b0, a1, b1):
    t = pl.program_id(0)
    n_blocks = pl.num_programs(0) - 1

    @pl.when(t % blocks_per_seq == 0)
    def _():
        state[...] = jnp.zeros_like(state)

    step = functools.partial(_mix_and_proj, chunk_decay, u_ref, vn_ref, az_ref, qk_ref, v_ref, rz_ref,
                             ga_ref, gb_ref, ws_ref, bsb_ref, woa_ref, wob_ref, m_ref, state)

    @pl.when(t == 0)
    def _():
        step((a0, b0), None)

    @pl.when((t > 0) & (t < n_blocks) & (t % 2 == 1))
    def _():
        step((a1, b1), (a0, b0))

    @pl.when((t > 0) & (t < n_blocks) & (t % 2 == 0))
    def _():
        step((a0, b0), (a1, b1))

    @pl.when(t == n_blocks)
    def _():
        step(None, (a1, b1))


def _mixproj(p, ws, bsb, woa, wob, chunk_decay, seq):
    tokens = p.shape[0]
    tb = MIX_ROWS
    n_blocks = tokens // tb
    assert n_blocks % 2 == 0
    last = n_blocks - 1

    def cur(jcol):
        return pl.BlockSpec((tb, D_MODEL), lambda t: (jnp.minimum(t, last), jcol))

    def prev(jcol):
        return pl.BlockSpec((tb, D_MODEL), lambda t: (jnp.maximum(t - 1, 0), jcol))

    def whole(a):
        nd = a.ndim
        return pl.BlockSpec(a.shape, lambda t: (0,) * nd)

    wspec = pl.BlockSpec((D_MODEL, D_MODEL), lambda t: (0, 0), pipeline_mode=pl.Buffered(1))
    blk = tb * D_MODEL
    vmem = (2 * D_MODEL * D_MODEL * 2 + 2 * 8 * blk * 2 + 2 * blk * 2 + 4 * blk * 2
            + R_HEADS * R_QK_DIM * R_V_DIM * 4 + 4 * blk * 4 + (4 << 20))
    return pl.pallas_call(
        functools.partial(_mixproj_kernel, chunk_decay, seq // tb),
        grid=(n_blocks + 1,),
        in_specs=[cur(0), cur(1), cur(2), cur(3), cur(4), cur(5), prev(6), prev(7),
                  whole(ws), whole(bsb), wspec, wspec],
        out_specs=pl.BlockSpec((tb, D_MODEL), lambda t: (jnp.maximum(t - 1, 0), 0)),
        out_shape=jax.ShapeDtypeStruct((tokens, D_MODEL), BF16),
        scratch_shapes=[pltpu.VMEM((R_HEADS, R_QK_DIM, R_V_DIM), F32)]
        + [pltpu.VMEM((tb, D_MODEL), BF16)] * 4,
        compiler_params=pltpu.CompilerParams(
            dimension_semantics=("arbitrary",),
            vmem_limit_bytes=min(vmem + (6 << 20), VMEM_CEILING)),
        name="mixproj",
    )(p, p, p, p, p, p, p, p, ws, bsb, woa, wob)


def _out_kernel(m_ref, x_ref, wout_ref, lng_ref, lnb_ref, o_ref):
    for q in range(m_ref.shape[0] // SLAB_ROWS):
        rows = slice(q * SLAB_ROWS, (q + 1) * SLAB_ROWS)
        out = jnp.dot(m_ref[rows, :], wout_ref[...], preferred_element_type=F32)
        y = _standardize_rows(np.float32(DEEPNORM_ALPHA) * x_ref[rows, :] + out)
        o_ref[rows, :] = y * lng_ref[...] + lnb_ref[...]


def _finish(m, x2, wout, ln_g, ln_b):
    tokens = x2.shape[0]
    tb = OUT_ROWS
    row = lambda i: (i, 0)
    const = lambda i: (0, 0)
    blk = tb * D_MODEL
    vmem = D_MODEL * D_MODEL * 2 + 2 * (blk * 2 + 2 * blk * 4) + 4 * SLAB_ROWS * D_MODEL * 4
    return pl.pallas_call(
        _out_kernel,
        grid=(tokens // tb,),
        in_specs=[
            pl.BlockSpec((tb, D_MODEL), row),
            pl.BlockSpec((tb, D_MODEL), row),
            pl.BlockSpec((D_MODEL, D_MODEL), const, pipeline_mode=pl.Buffered(1)),
            pl.BlockSpec((1, D_MODEL), const),
            pl.BlockSpec((1, D_MODEL), const),
        ],
        out_specs=pl.BlockSpec((tb, D_MODEL), row),
        out_shape=jax.ShapeDtypeStruct((tokens, D_MODEL), F32),
        compiler_params=pltpu.CompilerParams(
            dimension_semantics=("arbitrary",),
            vmem_limit_bytes=min(vmem + (6 << 20), VMEM_CEILING)),
        name="finish",
    )(m, x2, wout, ln_g, ln_b)


def _rotary_tables(seq):
    half = R_QK_DIM // 2
    freqs = ROPE_BASE ** (-jnp.arange(0, R_QK_DIM, 2, dtype=F32) / R_QK_DIM)
    ang = jnp.arange(seq, dtype=jnp.int32).astype(F32)[:, None] * freqs[None, :]
    cos, sin = jnp.cos(ang), jnp.sin(ang)
    assert cos.shape == (seq, half)
    return jnp.concatenate([cos, cos], axis=-1), jnp.concatenate([-sin, sin], axis=-1)


def _retention_constants():
    log_gamma = np.log1p(-np.exp2(-5.0 - np.arange(R_HEADS, dtype=np.float64)))
    pos = np.arange(CHUNK, dtype=np.float64) + 1.0
    q_fac = np.exp(log_gamma[None, :] * pos[:, None])
    k_fac = np.exp(-log_gamma[None, :] * pos[:, None]) * R_QK_DIM ** -0.5
    table = np.concatenate([np.repeat(q_fac, R_QK_DIM, axis=1), np.repeat(k_fac, R_QK_DIM, axis=1)], axis=1)
    chunk_decay = tuple(float(np.float32(v)) for v in np.exp(log_gamma * CHUNK))
    return jnp.asarray(table, F32), chunk_decay


def kernel(x, w_in, b_gate, ln_v_g, ln_v_b, w_s, b_s, w_oa, w_ob, w_out, ln_g, ln_b):
    batch, seq, d = x.shape
    assert d == D_MODEL and w_in.shape == (DEPTH, D_MODEL, IN_WIDTH)
    assert seq % PROJ_ROWS == 0 and seq % MIX_ROWS == 0 and (batch * seq) % OUT_ROWS == 0
    tokens = batch * seq
    x2 = x.reshape(tokens, D_MODEL)
    cos_t, sin_t = _rotary_tables(seq)
    dk_t, chunk_decay = _retention_constants()
    for l in range(DEPTH):
        p = _project(x2, w_in[l].astype(BF16), ln_v_g[l][None, :], ln_v_b[l][None, :],
                     b_gate[l].reshape(2, D_MODEL), cos_t, sin_t, dk_t, seq)
        bsb = jnp.broadcast_to(b_s[l][:, :, None], (A_GROUPS, CHUNK, CHUNK))
        m = _mixproj(p, w_s[l], bsb, w_oa[l].astype(BF16), w_ob[l].astype(BF16), chunk_decay, seq)
        x2 = _finish(m, x2, w_out[l].astype(BF16), ln_g[l][None, :], ln_b[l][None, :])
    return x2.reshape(batch, seq, D_MODEL)
```

```python
import functools

import numpy as np
import jax
import jax.numpy as jnp
from jax import lax
from jax.experimental import pallas as pl
from jax.experimental.pallas import tpu as pltpu

D_MODEL = 2048
CHUNK = 128
A_GROUPS = 8
A_GROUP_DIM = D_MODEL // A_GROUPS
R_HEADS = 8
R_QK_DIM = D_MODEL // (2 * R_HEADS)
R_V_DIM = D_MODEL // R_HEADS
R_QK_WIDTH = R_HEADS * R_QK_DIM
ROPE_BASE = 10000.0
LN_EPS = 1e-5
DEPTH = 1
DEEPNORM_ALPHA = (2 * DEPTH) ** 0.25
N_COL_GROUPS = 8
IN_WIDTH = N_COL_GROUPS * D_MODEL

V7X_VMEM_BYTES = 64 * 1024 * 1024
VMEM_CEILING = V7X_VMEM_BYTES - (6 << 20)

PROJ_ROWS = 1024
MIX_ROWS = 256
OUT_ROWS = 1024
SLAB_ROWS = 256

F32 = jnp.float32
BF16 = jnp.bfloat16


def _gelu_tanh(x):
    c = np.float32(np.sqrt(2.0 / np.pi))
    return 0.5 * x * (1.0 + jnp.tanh(c * (x + np.float32(0.044715) * (x * x * x))))


def _sigmoid(x):
    return 0.5 * (jnp.tanh(0.5 * x) + 1.0)


def _silu(x):
    return x * _sigmoid(x)


def _standardize_rows(x):
    mu = jnp.mean(x, axis=-1, keepdims=True)
    d = x - mu
    var = jnp.mean(d * d, axis=-1, keepdims=True)
    return d * lax.rsqrt(var + LN_EPS)


def _proj_kernel(x_ref, w_ref, lng_ref, lnb_ref, bg_ref, cos_ref, sin_ref, dk_ref, o_ref):
    j = pl.program_id(0)

    def per_slab(epilogue):
        for q in range(x_ref.shape[0] // SLAB_ROWS):
            rows = slice(q * SLAB_ROWS, (q + 1) * SLAB_ROWS)
            acc = jnp.dot(x_ref[rows, :].astype(BF16), w_ref[...], preferred_element_type=F32)
            epilogue(acc, rows)

    def store(fn):
        def epilogue(acc, rows):
            o_ref[rows, :] = fn(acc).astype(BF16)
        return epilogue

    @pl.when((j == 0))
    def _():
        per_slab(store(_gelu_tanh))

    @pl.when(j == 1)
    def _():
        per_slab(store(lambda acc: _standardize_rows(_gelu_tanh(acc)) * lng_ref[...] + lnb_ref[...]))

    @pl.when((j == 2) | (j == 5))
    def _():
        per_slab(store(_silu))

    @pl.when(j == 3)
    def _():
        def epilogue(acc, rows):
            cos = cos_ref[rows, :]
            sin = sin_ref[rows, :]
            for c in range(2 * R_HEADS):
                cols = slice(c * R_QK_DIM, (c + 1) * R_QK_DIM)
                blk = acc[:, cols]
                rot = blk * cos + pltpu.roll(blk, R_QK_DIM // 2, 1) * sin
                for rr in range(SLAB_ROWS // CHUNK):
                    sub = slice(rr * CHUNK, (rr + 1) * CHUNK)
                    dst = slice(rows.start + rr * CHUNK, rows.start + (rr + 1) * CHUNK)
                    o_ref[dst, cols] = (rot[sub] * dk_ref[:, cols]).astype(BF16)
        per_slab(epilogue)

    @pl.when(j == 4)
    def _():
        per_slab(store(lambda acc: acc))

    @pl.when(j == 6)
    def _():
        per_slab(store(lambda acc: _sigmoid(acc + bg_ref[0:1, :])))

    @pl.when(j == 7)
    def _():
        per_slab(store(lambda acc: _sigmoid(acc + bg_ref[1:2, :])))


def _project(xb, wb, ln_v_g, ln_v_b, b_gate2, cos_t, sin_t, dk_t, seq):
    tokens = xb.shape[0]
    bm = PROJ_ROWS
    n_i = tokens // bm
    pos_blocks = seq // bm
    vmem = (2 * (bm * D_MODEL * xb.dtype.itemsize + D_MODEL * D_MODEL * 2 + bm * D_MODEL * 2)
            + 4 * SLAB_ROWS * D_MODEL * 4)
    const = lambda j, i: (0, 0)
    return pl.pallas_call(
        _proj_kernel,
        grid=(N_COL_GROUPS, n_i),
        in_specs=[
            pl.BlockSpec((bm, D_MODEL), lambda j, i: (i, 0)),
            pl.BlockSpec((D_MODEL, D_MODEL), lambda j, i: (0, j)),
            pl.BlockSpec((1, D_MODEL), const),
            pl.BlockSpec((1, D_MODEL), const),
            pl.BlockSpec((2, D_MODEL), const),
            pl.BlockSpec((bm, R_QK_DIM), lambda j, i: (i % pos_blocks, 0)),
            pl.BlockSpec((bm, R_QK_DIM), lambda j, i: (i % pos_blocks, 0)),
            pl.BlockSpec((CHUNK, D_MODEL), const),
        ],
        out_specs=pl.BlockSpec((bm, D_MODEL), lambda j, i: (i, j)),
        out_shape=jax.ShapeDtypeStruct((tokens, IN_WIDTH), BF16),
        compiler_params=pltpu.CompilerParams(
            dimension_semantics=("arbitrary", "arbitrary"),
            vmem_limit_bytes=min(vmem + (8 << 20), VMEM_CEILING)),
        name="proj",
    )(xb, wb, ln_v_g, ln_v_b, b_gate2, cos_t, sin_t, dk_t)


def _mix_and_proj(chunk_decay, u_ref, vn_ref, az_ref, qk_ref, v_ref, rz_ref, ga_ref, gb_ref,
                  ws_ref, bsb_ref, woa_ref, wob_ref, m_ref, state, mix_to, proj_from):
    rows = u_ref.shape[0]
    heads = [(c, h) for c in range(rows // CHUNK) for h in range(R_HEADS)]
    n_tiles = D_MODEL // R_V_DIM
    assert len(heads) == 2 * n_tiles
    if mix_to is not None:
        a_scr, b_scr = mix_to
        t_idx = lax.broadcasted_iota(jnp.int32, (CHUNK, CHUNK), 0)
        s_idx = lax.broadcasted_iota(jnp.int32, (CHUNK, CHUNK), 1)
        causal = t_idx >= s_idx

    def qk(c, h):
        r = slice(c * CHUNK, (c + 1) * CHUNK)
        q = qk_ref[r, h * R_QK_DIM:(h + 1) * R_QK_DIM]
        k = qk_ref[r, R_QK_WIDTH + h * R_QK_DIM:R_QK_WIDTH + (h + 1) * R_QK_DIM]
        return lax.dot_general(q, k, (((1,), (1,)), ((), ())), preferred_element_type=F32)

    def spatial_gate(c, g):
        r = slice(c * CHUNK, (c + 1) * CHUNK)
        cs = slice(g * A_GROUP_DIM, (g + 1) * A_GROUP_DIM)
        ws_g = jnp.where(causal, ws_ref[g], 0.0).astype(BF16)
        sv = jnp.dot(ws_g, vn_ref[r, cs], preferred_element_type=F32)
        bias = bsb_ref[g]
        sv = jnp.concatenate([sv[:, :CHUNK] + bias, sv[:, CHUNK:] + bias], axis=1)
        a = u_ref[r, cs].astype(F32) * sv * az_ref[r, cs].astype(F32)
        a_scr[r, cs] = a.astype(BF16)

    def retention(c, h, s):
        r = slice(c * CHUNK, (c + 1) * CHUNK)
        q = qk_ref[r, h * R_QK_DIM:(h + 1) * R_QK_DIM]
        k = qk_ref[r, R_QK_WIDTH + h * R_QK_DIM:R_QK_WIDTH + (h + 1) * R_QK_DIM]
        vs = slice(h * R_V_DIM, (h + 1) * R_V_DIM)
        v = v_ref[r, vs]
        s = jnp.where(causal, s, 0.0).astype(BF16)
        st = state[h]
        lhs = jnp.concatenate([s, q], axis=1)
        rhs = jnp.concatenate([v, st.astype(BF16)], axis=0)
        o = _standardize_rows(jnp.dot(lhs, rhs, preferred_element_type=F32))
        b_scr[r, vs] = (o * rz_ref[r, vs].astype(F32)).astype(BF16)
        kv = lax.dot_general(k, v, (((0,), (0,)), ((), ())), preferred_element_type=F32)
        state[h] = (st + kv) * chunk_decay[h]

    ya = None
    scores = qk(*heads[0]) if mix_to is not None else None
    for idx, (c, h) in enumerate(heads):
        if mix_to is not None:
            nxt = qk(*heads[idx + 1]) if idx + 1 < len(heads) else None
            spatial_gate(c, h)
        if proj_from is not None:
            cols = slice((idx // 2) * R_V_DIM, (idx // 2 + 1) * R_V_DIM)
            if idx % 2 == 0:
                ya = jnp.dot(proj_from[0][...], woa_ref[:, cols], preferred_element_type=F32)
            else:
                yb = jnp.dot(proj_from[1][...], wob_ref[:, cols], preferred_element_type=F32)
                merged = ga_ref[:, cols].astype(F32) * ya + gb_ref[:, cols].astype(F32) * yb
                m_ref[:, cols] = merged.astype(BF16)
        if mix_to is not None:
            retention(c, h, scores)
            scores = nxt


def _mixproj_kernel(chunk_decay, blocks_per_seq,
                    u_ref, vn_ref, az_ref, qk_ref, v_ref, rz_ref, ga_ref, gb_ref,
                    ws_ref, bsb_ref, woa_ref, wob_ref, m_ref,
                    state, a0, b0, a1, b1):
    t = pl.program_id(0)
    n_blocks = pl.num_programs(0) - 1

    @pl.when(t % blocks_per_seq == 0)
    def _():
        state[...] = jnp.zeros_like(state)

    step = functools.partial(_mix_and_proj, chunk_decay, u_ref, vn_ref, az_ref, qk_ref, v_ref, rz_ref,
                             ga_ref, gb_ref, ws_ref, bsb_ref, woa_ref, wob_ref, m_ref, state)

    @pl.when(t == 0)
    def _():
        step((a0, b0), None)

    @pl.when((t > 0) & (t < n_blocks) & (t % 2 == 1))
    def _():
        step((a1, b1), (a0, b0))

    @pl.when((t > 0) & (t < n_blocks) & (t % 2 == 0))
    def _():
        step((a0, b0), (a1, b1))

    @pl.when(t == n_blocks)
    def _():
        step(None, (a1, b1))


def _mixproj(p, ws, bsb, woa, wob, chunk_decay, seq):
    tokens = p.shape[0]
    tb = MIX_ROWS
    n_blocks = tokens // tb
    assert n_blocks % 2 == 0
    last = n_blocks - 1

    def cur(jcol):
        return pl.BlockSpec((tb, D_MODEL), lambda t: (jnp.minimum(t, last), jcol))

    def prev(jcol):
        return pl.BlockSpec((tb, D_MODEL), lambda t: (jnp.maximum(t - 1, 0), jcol))

    def whole(a):
        nd = a.ndim
        return pl.BlockSpec(a.shape, lambda t: (0,) * nd)

    wspec = pl.BlockSpec((D_MODEL, D_MODEL), lambda t: (0, 0), pipeline_mode=pl.Buffered(1))
    blk = tb * D_MODEL
    vmem = (2 * D_MODEL * D_MODEL * 2 + 2 * 8 * blk * 2 + 2 * blk * 2 + 4 * blk * 2
            + R_HEADS * R_QK_DIM * R_V_DIM * 4 + 4 * blk * 4 + (4 << 20))
    return pl.pallas_call(
        functools.partial(_mixproj_kernel, chunk_decay, seq // tb),
        grid=(n_blocks + 1,),
        in_specs=[cur(0), cur(1), cur(2), cur(3), cur(4), cur(5), prev(6), prev(7),
                  whole(ws), whole(bsb), wspec, wspec],
        out_specs=pl.BlockSpec((tb, D_MODEL), lambda t: (jnp.maximum(t - 1, 0), 0)),
        out_shape=jax.ShapeDtypeStruct((tokens, D_MODEL), BF16),
        scratch_shapes=[pltpu.VMEM((R_HEADS, R_QK_DIM, R_V_DIM), F32)]
        + [pltpu.VMEM((tb, D_MODEL), BF16)] * 4,
        compiler_params=pltpu.CompilerParams(
            dimension_semantics=("arbitrary",),
            vmem_limit_bytes=min(vmem + (6 << 20), VMEM_CEILING)),
        name="mixproj",
    )(p, p, p, p, p, p, p, p, ws, bsb, woa, wob)


def _out_kernel(m_ref, x_ref, wout_ref, lng_ref, lnb_ref, o_ref):
    for q in range(m_ref.shape[0] // SLAB_ROWS):
        rows = slice(q * SLAB_ROWS, (q + 1) * SLAB_ROWS)
        out = jnp.dot(m_ref[rows, :], wout_ref[...], preferred_element_type=F32)
        y = _standardize_rows(np.float32(DEEPNORM_ALPHA) * x_ref[rows, :] + out)
        o_ref[rows, :] = y * lng_ref[...] + lnb_ref[...]


def _finish(m, x2, wout, ln_g, ln_b):
    tokens = x2.shape[0]
    tb = OUT_ROWS
    row = lambda i: (i, 0)
    const = lambda i: (0, 0)
    blk = tb * D_MODEL
    vmem = D_MODEL * D_MODEL * 2 + 2 * (blk * 2 + 2 * blk * 4) + 4 * SLAB_ROWS * D_MODEL * 4
    return pl.pallas_call(
        _out_kernel,
        grid=(tokens // tb,),
        in_specs=[
            pl.BlockSpec((tb, D_MODEL), row),
            pl.BlockSpec((tb, D_MODEL), row),
            pl.BlockSpec((D_MODEL, D_MODEL), const, pipeline_mode=pl.Buffered(1)),
            pl.BlockSpec((1, D_MODEL), const),
            pl.BlockSpec((1, D_MODEL), const),
        ],
        out_specs=pl.BlockSpec((tb, D_MODEL), row),
        out_shape=jax.ShapeDtypeStruct((tokens, D_MODEL), F32),
        compiler_params=pltpu.CompilerParams(
            dimension_semantics=("arbitrary",),
            vmem_limit_bytes=min(vmem + (6 << 20), VMEM_CEILING)),
        name="finish",
    )(m, x2, wout, ln_g, ln_b)


def _rotary_tables(seq):
    half = R_QK_DIM // 2
    freqs = ROPE_BASE ** (-jnp.arange(0, R_QK_DIM, 2, dtype=F32) / R_QK_DIM)
    ang = jnp.arange(seq, dtype=jnp.int32).astype(F32)[:, None] * freqs[None, :]
    cos, sin = jnp.cos(ang), jnp.sin(ang)
    assert cos.shape == (seq, half)
    return jnp.concatenate([cos, cos], axis=-1), jnp.concatenate([-sin, sin], axis=-1)


def _retention_constants():
    log_gamma = np.log1p(-np.exp2(-5.0 - np.arange(R_HEADS, dtype=np.float64)))
    pos = np.arange(CHUNK, dtype=np.float64) + 1.0
    q_fac = np.exp(log_gamma[None, :] * pos[:, None])
    k_fac = np.exp(-log_gamma[None, :] * pos[:, None]) * R_QK_DIM ** -0.5
    table = np.concatenate([np.repeat(q_fac, R_QK_DIM, axis=1), np.repeat(k_fac, R_QK_DIM, axis=1)], axis=1)
    chunk_decay = tuple(float(np.float32(v)) for v in np.exp(log_gamma * CHUNK))
    return jnp.asarray(table, F32), chunk_decay


def kernel(x, w_in, b_gate, ln_v_g, ln_v_b, w_s, b_s, w_oa, w_ob, w_out, ln_g, ln_b):
    batch, seq, d = x.shape
    assert d == D_MODEL and w_in.shape == (DEPTH, D_MODEL, IN_WIDTH)
    assert seq % PROJ_ROWS == 0 and seq % MIX_ROWS == 0 and (batch * seq) % OUT_ROWS == 0
    tokens = batch * seq
    x2 = x.reshape(tokens, D_MODEL)
    cos_t, sin_t = _rotary_tables(seq)
    dk_t, chunk_decay = _retention_constants()
    for l in range(DEPTH):
        p = _project(x2.astype(BF16), w_in[l].astype(BF16), ln_v_g[l][None, :], ln_v_b[l][None, :],
                     b_gate[l].reshape(2, D_MODEL), cos_t, sin_t, dk_t, seq)
        bsb = jnp.broadcast_to(b_s[l][:, :, None], (A_GROUPS, CHUNK, CHUNK))
        m = _mixproj(p, w_s[l], bsb, w_oa[l].astype(BF16), w_ob[l].astype(BF16), chunk_decay, seq)
        x2 = _finish(m, x2, w_out[l].astype(BF16), ln_g[l][None, :], ln_b[l][None, :])
    return x2.reshape(batch, seq, D_MODEL)
```

```python
import functools

import numpy as np
import jax
import jax.numpy as jnp
from jax import lax
from jax.experimental import pallas as pl
from jax.experimental.pallas import tpu as pltpu

D_MODEL = 2048
CHUNK = 128
A_GROUPS = 8
A_GROUP_DIM = D_MODEL // A_GROUPS
R_HEADS = 8
R_QK_DIM = D_MODEL // (2 * R_HEADS)
R_V_DIM = D_MODEL // R_HEADS
R_QK_WIDTH = R_HEADS * R_QK_DIM
ROPE_BASE = 10000.0
LN_EPS = 1e-5
DEPTH = 1
DEEPNORM_ALPHA = (2 * DEPTH) ** 0.25
N_COL_GROUPS = 8
IN_WIDTH = N_COL_GROUPS * D_MODEL

V7X_VMEM_BYTES = 64 * 1024 * 1024
VMEM_CEILING = V7X_VMEM_BYTES - (6 << 20)

PROJ_ROWS = 1024
LOOP_ROWS = 512
MIX_ROWS = 256
OUT_ROWS = 1024
SLAB_ROWS = 256

F32 = jnp.float32
BF16 = jnp.bfloat16


def _gelu_tanh(x):
    c = np.float32(np.sqrt(2.0 / np.pi))
    ca = np.float32(np.sqrt(2.0 / np.pi) * 0.044715)
    hx = 0.5 * x
    return hx + hx * jnp.tanh(x * (c + ca * (x * x)))


def _sigmoid(x):
    return 0.5 * (jnp.tanh(0.5 * x) + 1.0)


def _silu(x):
    hx = 0.5 * x
    return hx + hx * jnp.tanh(hx)


def _standardize_rows(x):
    mu = jnp.mean(x, axis=-1, keepdims=True)
    d = x - mu
    var = jnp.mean(d * d, axis=-1, keepdims=True)
    return d * lax.rsqrt(var + LN_EPS)


def _proj_kernel(x_ref, w_ref, lng_ref, lnb_ref, bg_ref, cos_ref, sin_ref, dk_ref, o_ref):
    j = pl.program_id(0)

    def per_slab(epilogue):
        def body(part, carry):
            for q in range(LOOP_ROWS // SLAB_ROWS):
                start = pl.multiple_of(part * LOOP_ROWS + q * SLAB_ROWS, SLAB_ROWS)
                lhs = x_ref[pl.ds(start, SLAB_ROWS), :].astype(BF16)
                epilogue(jnp.dot(lhs, w_ref[...], preferred_element_type=F32), start)
            return carry
        lax.fori_loop(0, x_ref.shape[0] // LOOP_ROWS, body, 0)

    def store(fn):
        def epilogue(acc, start):
            o_ref[pl.ds(start, SLAB_ROWS), :] = fn(acc).astype(BF16)
        return epilogue

    @pl.when((j == 0))
    def _():
        per_slab(store(_gelu_tanh))

    @pl.when(j == 1)
    def _():
        per_slab(store(lambda acc: _standardize_rows(_gelu_tanh(acc)) * lng_ref[...] + lnb_ref[...]))

    @pl.when((j == 2) | (j == 5))
    def _():
        per_slab(store(_silu))

    @pl.when(j == 3)
    def _():
        def epilogue(acc, start):
            cos = cos_ref[pl.ds(start, SLAB_ROWS), :]
            sin = sin_ref[pl.ds(start, SLAB_ROWS), :]
            for c in range(2 * R_HEADS):
                cols = slice(c * R_QK_DIM, (c + 1) * R_QK_DIM)
                blk = acc[:, cols]
                rot = blk * cos + pltpu.roll(blk, R_QK_DIM // 2, 1) * sin
                for rr in range(SLAB_ROWS // CHUNK):
                    sub = slice(rr * CHUNK, (rr + 1) * CHUNK)
                    dst = pl.ds(pl.multiple_of(start + rr * CHUNK, CHUNK), CHUNK)
                    o_ref[dst, cols] = (rot[sub] * dk_ref[:, cols]).astype(BF16)
        per_slab(epilogue)

    @pl.when(j == 4)
    def _():
        per_slab(store(lambda acc: acc))

    @pl.when(j >= 6)
    def _():
        bias = bg_ref[pl.ds(j - 6, 1), :]
        per_slab(store(lambda acc: _sigmoid(acc + bias)))


def _project(xb, wb, ln_v_g, ln_v_b, b_gate2, cos_t, sin_t, dk_t, seq):
    tokens = xb.shape[0]
    bm = PROJ_ROWS
    n_i = tokens // bm
    pos_blocks = seq // bm
    vmem = (2 * (bm * D_MODEL * xb.dtype.itemsize + D_MODEL * D_MODEL * 2 + bm * D_MODEL * 2)
            + 4 * SLAB_ROWS * D_MODEL * 4)
    const = lambda j, i: (0, 0)
    return pl.pallas_call(
        _proj_kernel,
        grid=(N_COL_GROUPS, n_i),
        in_specs=[
            pl.BlockSpec((bm, D_MODEL), lambda j, i: (i, 0)),
            pl.BlockSpec((D_MODEL, D_MODEL), lambda j, i: (0, j)),
            pl.BlockSpec((1, D_MODEL), const),
            pl.BlockSpec((1, D_MODEL), const),
            pl.BlockSpec((2, D_MODEL), const),
            pl.BlockSpec((bm, R_QK_DIM), lambda j, i: (i % pos_blocks, 0)),
            pl.BlockSpec((bm, R_QK_DIM), lambda j, i: (i % pos_blocks, 0)),
            pl.BlockSpec((CHUNK, D_MODEL), const),
        ],
        out_specs=pl.BlockSpec((bm, D_MODEL), lambda j, i: (i, j)),
        out_shape=jax.ShapeDtypeStruct((tokens, IN_WIDTH), BF16),
        compiler_params=pltpu.CompilerParams(
            dimension_semantics=("arbitrary", "arbitrary"),
            vmem_limit_bytes=min(vmem + (8 << 20), VMEM_CEILING)),
        name="proj",
    )(xb, wb, ln_v_g, ln_v_b, b_gate2, cos_t, sin_t, dk_t)


def _mix_and_proj(chunk_decay, u_ref, vn_ref, az_ref, qk_ref, v_ref, rz_ref, ga_ref, gb_ref,
                  ws_ref, bsb_ref, woa_ref, wob_ref, m_ref, state, mix_to, proj_from):
    rows = u_ref.shape[0]
    heads = [(c, h) for c in range(rows // CHUNK) for h in range(R_HEADS)]
    n_tiles = D_MODEL // R_V_DIM
    assert len(heads) == 2 * n_tiles
    if mix_to is not None:
        a_scr, b_scr = mix_to
        t_idx = lax.broadcasted_iota(jnp.int32, (CHUNK, CHUNK), 0)
        s_idx = lax.broadcasted_iota(jnp.int32, (CHUNK, CHUNK), 1)
        causal = t_idx >= s_idx

    def qk(c, h):
        r = slice(c * CHUNK, (c + 1) * CHUNK)
        q = qk_ref[r, h * R_QK_DIM:(h + 1) * R_QK_DIM]
        k = qk_ref[r, R_QK_WIDTH + h * R_QK_DIM:R_QK_WIDTH + (h + 1) * R_QK_DIM]
        return lax.dot_general(q, k, (((1,), (1,)), ((), ())), preferred_element_type=F32)

    def spatial_gate(c, g):
        r = slice(c * CHUNK, (c + 1) * CHUNK)
        cs = slice(g * A_GROUP_DIM, (g + 1) * A_GROUP_DIM)
        ws_g = jnp.where(causal, ws_ref[g], 0.0).astype(BF16)
        sv = jnp.dot(ws_g, vn_ref[r, cs], preferred_element_type=F32)
        bias = bsb_ref[g]
        sv = jnp.concatenate([sv[:, :CHUNK] + bias, sv[:, CHUNK:] + bias], axis=1)
        a = u_ref[r, cs].astype(F32) * sv * az_ref[r, cs].astype(F32)
        a_scr[r, cs] = a.astype(BF16)

    def retention(c, h, s):
        r = slice(c * CHUNK, (c + 1) * CHUNK)
        q = qk_ref[r, h * R_QK_DIM:(h + 1) * R_QK_DIM]
        k = qk_ref[r, R_QK_WIDTH + h * R_QK_DIM:R_QK_WIDTH + (h + 1) * R_QK_DIM]
        vs = slice(h * R_V_DIM, (h + 1) * R_V_DIM)
        v = v_ref[r, vs]
        s = jnp.where(causal, s, 0.0).astype(BF16)
        st = state[h]
        lhs = jnp.concatenate([s, q], axis=1)
        rhs = jnp.concatenate([v, st.astype(BF16)], axis=0)
        o = _standardize_rows(jnp.dot(lhs, rhs, preferred_element_type=F32))
        b_scr[r, vs] = (o * rz_ref[r, vs].astype(F32)).astype(BF16)
        kv = lax.dot_general(k, v, (((0,), (0,)), ((), ())), preferred_element_type=F32)
        state[h] = (st + kv) * chunk_decay[h]

    ya = None
    scores = qk(*heads[0]) if mix_to is not None else None
    for idx, (c, h) in enumerate(heads):
        if mix_to is not None:
            nxt = qk(*heads[idx + 1]) if idx + 1 < len(heads) else None
            spatial_gate(c, h)
            retention(c, h, scores)
            scores = nxt
        if proj_from is not None:
            cols = slice((idx // 2) * R_V_DIM, (idx // 2 + 1) * R_V_DIM)
            if idx % 2 == 0:
                ya = jnp.dot(proj_from[0][...], woa_ref[:, cols], preferred_element_type=F32)
            else:
                yb = jnp.dot(proj_from[1][...], wob_ref[:, cols], preferred_element_type=F32)
                merged = ga_ref[:, cols].astype(F32) * ya + gb_ref[:, cols].astype(F32) * yb
                m_ref[:, cols] = merged.astype(BF16)


def _mixproj_kernel(chunk_decay, blocks_per_seq,
                    u_ref, vn_ref, az_ref, qk_ref, v_ref, rz_ref, ga_ref, gb_ref,
                    ws_ref, bsb_ref, woa_ref, wob_ref, m_ref,
                    state, a0, b0, a1, b1):
    t = pl.program_id(0)
    n_blocks = pl.num_programs(0) - 1

    @pl.when(t % blocks_per_seq == 0)
    def _():
        state[...] = jnp.zeros_like(state)

    step = functools.partial(_mix_and_proj, chunk_decay, u_ref, vn_ref, az_ref, qk_ref, v_ref, rz_ref,
                             ga_ref, gb_ref, ws_ref, bsb_ref, woa_ref, wob_ref, m_ref, state)

    @pl.when(t == 0)
    def _():
        step((a0, b0), None)

    @pl.when((t > 0) & (t < n_blocks) & (t % 2 == 1))
    def _():
        step((a1, b1), (a0, b0))

    @pl.when((t > 0) & (t < n_blocks) & (t % 2 == 0))
    def _():
        step((a0, b0), (a1, b1))

    @pl.when(t == n_blocks)
    def _():
        step(None, (a1, b1))


def _mixproj(p, ws, bsb, woa, wob, chunk_decay, seq):
    tokens = p.shape[0]
    tb = MIX_ROWS
    n_blocks = tokens // tb
    assert n_blocks % 2 == 0
    last = n_blocks - 1

    def cur(jcol):
        return pl.BlockSpec((tb, D_MODEL), lambda t: (jnp.minimum(t, last), jcol))

    def prev(jcol):
        return pl.BlockSpec((tb, D_MODEL), lambda t: (jnp.maximum(t - 1, 0), jcol))

    def whole(a):
        nd = a.ndim
        return pl.BlockSpec(a.shape, lambda t: (0,) * nd)

    wspec = pl.BlockSpec((D_MODEL, D_MODEL), lambda t: (0, 0), pipeline_mode=pl.Buffered(1))
    blk = tb * D_MODEL
    vmem = (2 * D_MODEL * D_MODEL * 2 + 2 * 8 * blk * 2 + 2 * blk * 2 + 4 * blk * 2
            + R_HEADS * R_QK_DIM * R_V_DIM * 4 + 4 * blk * 4 + (4 << 20))
    return pl.pallas_call(
        functools.partial(_mixproj_kernel, chunk_decay, seq // tb),
        grid=(n_blocks + 1,),
        in_specs=[cur(0), cur(1), cur(2), cur(3), cur(4), cur(5), prev(6), prev(7),
                  whole(ws), whole(bsb), wspec, wspec],
        out_specs=pl.BlockSpec((tb, D_MODEL), lambda t: (jnp.maximum(t - 1, 0), 0)),
        out_shape=jax.ShapeDtypeStruct((tokens, D_MODEL), BF16),
        scratch_shapes=[pltpu.VMEM((R_HEADS, R_QK_DIM, R_V_DIM), F32)]
        + [pltpu.VMEM((tb, D_MODEL), BF16)] * 4,
        compiler_params=pltpu.CompilerParams(
            dimension_semantics=("arbitrary",),
            vmem_limit_bytes=min(vmem + (6 << 20), VMEM_CEILING)),
        name="mixproj",
    )(p, p, p, p, p, p, p, p, ws, bsb, woa, wob)


def _out_kernel(m_ref, x_ref, wout_ref, lng_ref, lnb_ref, o_ref):
    for q in range(m_ref.shape[0] // SLAB_ROWS):
        rows = slice(q * SLAB_ROWS, (q + 1) * SLAB_ROWS)
        out = jnp.dot(m_ref[rows, :], wout_ref[...], preferred_element_type=F32)
        y = _standardize_rows(np.float32(DEEPNORM_ALPHA) * x_ref[rows, :] + out)
        o_ref[rows, :] = y * lng_ref[...] + lnb_ref[...]


def _finish(m, x2, wout, ln_g, ln_b):
    tokens = x2.shape[0]
    tb = OUT_ROWS
    row = lambda i: (i, 0)
    const = lambda i: (0, 0)
    blk = tb * D_MODEL
    vmem = D_MODEL * D_MODEL * 2 + 2 * (blk * 2 + 2 * blk * 4) + 4 * SLAB_ROWS * D_MODEL * 4
    return pl.pallas_call(
        _out_kernel,
        grid=(tokens // tb,),
        in_specs=[
            pl.BlockSpec((tb, D_MODEL), row),
            pl.BlockSpec((tb, D_MODEL), row),
            pl.BlockSpec((D_MODEL, D_MODEL), const, pipeline_mode=pl.Buffered(1)),
            pl.BlockSpec((1, D_MODEL), const),
            pl.BlockSpec((1, D_MODEL), const),
        ],
        out_specs=pl.BlockSpec((tb, D_MODEL), row),
        out_shape=jax.ShapeDtypeStruct((tokens, D_MODEL), F32),
        compiler_params=pltpu.CompilerParams(
            dimension_semantics=("arbitrary",),
            vmem_limit_bytes=min(vmem + (6 << 20), VMEM_CEILING)),
        name="finish",
    )(m, x2, wout, ln_g, ln_b)


def _rotary_tables(seq):
    half = R_QK_DIM // 2
    freqs = ROPE_BASE ** (-jnp.arange(0, R_QK_DIM, 2, dtype=F32) / R_QK_DIM)
    ang = jnp.arange(seq, dtype=jnp.int32).astype(F32)[:, None] * freqs[None, :]
    cos, sin = jnp.cos(ang), jnp.sin(ang)
    assert cos.shape == (seq, half)
    return jnp.concatenate([cos, cos], axis=-1), jnp.concatenate([-sin, sin], axis=-1)


def _retention_constants():
    log_gamma = np.log1p(-np.exp2(-5.0 - np.arange(R_HEADS, dtype=np.float64)))
    pos = np.arange(CHUNK, dtype=np.float64) + 1.0
    q_fac = np.exp(log_gamma[None, :] * pos[:, None])
    k_fac = np.exp(-log_gamma[None, :] * pos[:, None]) * R_QK_DIM ** -0.5
    table = np.concatenate([np.repeat(q_fac, R_QK_DIM, axis=1), np.repeat(k_fac, R_QK_DIM, axis=1)], axis=1)
    chunk_decay = tuple(float(np.float32(v)) for v in np.exp(log_gamma * CHUNK))
    return jnp.asarray(table, F32), chunk_decay


def kernel(x, w_in, b_gate, ln_v_g, ln_v_b, w_s, b_s, w_oa, w_ob, w_out, ln_g, ln_b):
    batch, seq, d = x.shape
    assert d == D_MODEL and w_in.shape == (DEPTH, D_MODEL, IN_WIDTH)
    assert seq % PROJ_ROWS == 0 and seq % MIX_ROWS == 0 and (batch * seq) % OUT_ROWS == 0
    tokens = batch * seq
    x2 = x.reshape(tokens, D_MODEL)
    cos_t, sin_t = _rotary_tables(seq)
    dk_t, chunk_decay = _retention_constants()
    for l in range(DEPTH):
        p = _project(x2, w_in[l].astype(BF16), ln_v_g[l][None, :], ln_v_b[l][None, :],
                     b_gate[l].reshape(2, D_MODEL), cos_t, sin_t, dk_t, seq)
        bsb = jnp.broadcast_to(b_s[l][:, :, None], (A_GROUPS, CHUNK, CHUNK))
        m = _mixproj(p, w_s[l], bsb, w_oa[l].astype(BF16), w_ob[l].astype(BF16), chunk_decay, seq)
        x2 = _finish(m, x2, w_out[l].astype(BF16), ln_g[l][None, :], ln_b[l][None, :])
    return x2.reshape(batch, seq, D_MODEL)
```

```python
import functools

import numpy as np
import jax
import jax.numpy as jnp
from jax import lax
from jax.experimental import pallas as pl
from jax.experimental.pallas import tpu as pltpu

D_MODEL = 2048
CHUNK = 128
A_GROUPS = 8
A_GROUP_DIM = D_MODEL // A_GROUPS
R_HEADS = 8
R_QK_DIM = D_MODEL // (2 * R_HEADS)
R_V_DIM = D_MODEL // R_HEADS
R_QK_WIDTH = R_HEADS * R_QK_DIM
ROPE_BASE = 10000.0
LN_EPS = 1e-5
DEPTH = 1
DEEPNORM_ALPHA = (2 * DEPTH) ** 0.25
N_COL_GROUPS = 8
IN_WIDTH = N_COL_GROUPS * D_MODEL

V7X_VMEM_BYTES = 64 * 1024 * 1024
VMEM_CEILING = V7X_VMEM_BYTES - (6 << 20)

PROJ_ROWS = 1024
LOOP_ROWS = 512
W_SLABS = 8
MIX_ROWS = 256
OUT_ROWS = 1024
SLAB_ROWS = 256

F32 = jnp.float32
BF16 = jnp.bfloat16


def _gelu_tanh(x):
    c = np.float32(np.sqrt(2.0 / np.pi))
    ca = np.float32(np.sqrt(2.0 / np.pi) * 0.044715)
    hx = 0.5 * x
    return hx + hx * jnp.tanh(x * (c + ca * (x * x)))


def _sigmoid(x):
    return 0.5 * (jnp.tanh(0.5 * x) + 1.0)


def _silu(x):
    hx = 0.5 * x
    return hx + hx * jnp.tanh(hx)


def _standardize_rows(x):
    mu = jnp.mean(x, axis=-1, keepdims=True)
    d = x - mu
    var = jnp.mean(d * d, axis=-1, keepdims=True)
    return d * lax.rsqrt(var + LN_EPS)


def _proj_kernel(n_w, x_ref, w_ref, lng_ref, lnb_ref, bg_ref, cos_ref, sin_ref, dk_ref, o_ref, wbuf):
    jj = pl.program_id(0)
    i = pl.program_id(1)
    j = jj - 1
    w_rows = w_ref.shape[0]

    @pl.when((jj < N_COL_GROUPS) & (i < n_w))
    def _():
        dst = pl.ds(pl.multiple_of(i * w_rows, w_rows), w_rows)
        wbuf[jj % 2, dst, :] = w_ref[...].astype(BF16)

    def per_slab(epilogue):
        slot = j % 2

        def body(part, carry):
            for q in range(LOOP_ROWS // SLAB_ROWS):
                start = pl.multiple_of(part * LOOP_ROWS + q * SLAB_ROWS, SLAB_ROWS)
                lhs = x_ref[pl.ds(start, SLAB_ROWS), :].astype(BF16)
                epilogue(jnp.dot(lhs, wbuf[slot], preferred_element_type=F32), start)
            return carry
        lax.fori_loop(0, x_ref.shape[0] // LOOP_ROWS, body, 0)

    def store(fn):
        def epilogue(acc, start):
            o_ref[pl.ds(start, SLAB_ROWS), :] = fn(acc).astype(BF16)
        return epilogue

    @pl.when((j == 0))
    def _():
        per_slab(store(_gelu_tanh))

    @pl.when(j == 1)
    def _():
        per_slab(store(lambda acc: _standardize_rows(_gelu_tanh(acc)) * lng_ref[...] + lnb_ref[...]))

    @pl.when((j == 2) | (j == 5))
    def _():
        per_slab(store(_silu))

    @pl.when(j == 3)
    def _():
        def epilogue(acc, start):
            cos = cos_ref[pl.ds(start, SLAB_ROWS), :]
            sin = sin_ref[pl.ds(start, SLAB_ROWS), :]
            for c in range(2 * R_HEADS):
                cols = slice(c * R_QK_DIM, (c + 1) * R_QK_DIM)
                blk = acc[:, cols]
                rot = blk * cos + pltpu.roll(blk, R_QK_DIM // 2, 1) * sin
                for rr in range(SLAB_ROWS // CHUNK):
                    sub = slice(rr * CHUNK, (rr + 1) * CHUNK)
                    dst = pl.ds(pl.multiple_of(start + rr * CHUNK, CHUNK), CHUNK)
                    o_ref[dst, cols] = (rot[sub] * dk_ref[:, cols]).astype(BF16)
        per_slab(epilogue)

    @pl.when(j == 4)
    def _():
        per_slab(store(lambda acc: acc))

    @pl.when(j >= 6)
    def _():
        bias = bg_ref[pl.ds(j - 6, 1), :]
        per_slab(store(lambda acc: _sigmoid(acc + bias)))


def _project(x2, w, ln_v_g, ln_v_b, b_gate2, cos_t, sin_t, dk_t, seq):
    tokens = x2.shape[0]
    bm = PROJ_ROWS
    n_i = tokens // bm
    pos_blocks = seq // bm
    n_w = min(n_i, W_SLABS)
    w_rows = D_MODEL // n_w
    last_group = N_COL_GROUPS - 1
    vmem = (2 * (bm * D_MODEL * 4 + w_rows * D_MODEL * 4 + bm * D_MODEL * 2)
            + 2 * D_MODEL * D_MODEL * 2 + 4 * SLAB_ROWS * D_MODEL * 4)
    const = lambda jj, i: (0, 0)
    rows = lambda jj, i: jnp.where(jj == 0, 0, i)
    w_slab = lambda jj, i: jnp.where(jj > last_group, n_w - 1, jnp.minimum(i, n_w - 1))
    return pl.pallas_call(
        functools.partial(_proj_kernel, n_w),
        grid=(N_COL_GROUPS + 1, n_i),
        in_specs=[
            pl.BlockSpec((bm, D_MODEL), lambda jj, i: (rows(jj, i), 0)),
            pl.BlockSpec((w_rows, D_MODEL), lambda jj, i: (w_slab(jj, i), jnp.minimum(jj, last_group))),
            pl.BlockSpec((1, D_MODEL), const),
            pl.BlockSpec((1, D_MODEL), const),
            pl.BlockSpec((2, D_MODEL), const),
            pl.BlockSpec((bm, R_QK_DIM), lambda jj, i: (i % pos_blocks, 0)),
            pl.BlockSpec((bm, R_QK_DIM), lambda jj, i: (i % pos_blocks, 0)),
            pl.BlockSpec((CHUNK, D_MODEL), const),
        ],
        out_specs=pl.BlockSpec((bm, D_MODEL), lambda jj, i: (rows(jj, i), jnp.maximum(jj - 1, 0))),
        out_shape=jax.ShapeDtypeStruct((tokens, IN_WIDTH), BF16),
        scratch_shapes=[pltpu.VMEM((2, D_MODEL, D_MODEL), BF16)],
        compiler_params=pltpu.CompilerParams(
            dimension_semantics=("arbitrary", "arbitrary"),
            vmem_limit_bytes=min(vmem + (8 << 20), VMEM_CEILING)),
        name="proj",
    )(x2, w, ln_v_g, ln_v_b, b_gate2, cos_t, sin_t, dk_t)


def _mix_and_proj(chunk_decay, u_ref, vn_ref, az_ref, qk_ref, v_ref, rz_ref, ga_ref, gb_ref,
                  ws_ref, bsb_ref, woa_ref, wob_ref, m_ref, state, mix_to, proj_from):
    rows = u_ref.shape[0]
    heads = [(c, h) for c in range(rows // CHUNK) for h in range(R_HEADS)]
    n_tiles = D_MODEL // R_V_DIM
    assert len(heads) == 2 * n_tiles
    if mix_to is not None:
        a_scr, b_scr = mix_to
        t_idx = lax.broadcasted_iota(jnp.int32, (CHUNK, CHUNK), 0)
        s_idx = lax.broadcasted_iota(jnp.int32, (CHUNK, CHUNK), 1)
        causal = t_idx >= s_idx

    def qk(c, h):
        r = slice(c * CHUNK, (c + 1) * CHUNK)
        q = qk_ref[r, h * R_QK_DIM:(h + 1) * R_QK_DIM]
        k = qk_ref[r, R_QK_WIDTH + h * R_QK_DIM:R_QK_WIDTH + (h + 1) * R_QK_DIM]
        return lax.dot_general(q, k, (((1,), (1,)), ((), ())), preferred_element_type=F32)

    def spatial_gate(c, g):
        r = slice(c * CHUNK, (c + 1) * CHUNK)
        cs = slice(g * A_GROUP_DIM, (g + 1) * A_GROUP_DIM)
        ws_g = jnp.where(causal, ws_ref[g], 0.0).astype(BF16)
        sv = jnp.dot(ws_g, vn_ref[r, cs], preferred_element_type=F32)
        bias = bsb_ref[g]
        sv = jnp.concatenate([sv[:, :CHUNK] + bias, sv[:, CHUNK:] + bias], axis=1)
        a = u_ref[r, cs].astype(F32) * sv * az_ref[r, cs].astype(F32)
        a_scr[r, cs] = a.astype(BF16)

    def retention(c, h, s):
        r = slice(c * CHUNK, (c + 1) * CHUNK)
        q = qk_ref[r, h * R_QK_DIM:(h + 1) * R_QK_DIM]
        k = qk_ref[r, R_QK_WIDTH + h * R_QK_DIM:R_QK_WIDTH + (h + 1) * R_QK_DIM]
        vs = slice(h * R_V_DIM, (h + 1) * R_V_DIM)
        v = v_ref[r, vs]
        s = jnp.where(causal, s, 0.0).astype(BF16)
        st = state[h]
        lhs = jnp.concatenate([s, q], axis=1)
        rhs = jnp.concatenate([v, st.astype(BF16)], axis=0)
        o = _standardize_rows(jnp.dot(lhs, rhs, preferred_element_type=F32))
        b_scr[r, vs] = (o * rz_ref[r, vs].astype(F32)).astype(BF16)
        kv = lax.dot_general(k, v, (((0,), (0,)), ((), ())), preferred_element_type=F32)
        state[h] = (st + kv) * chunk_decay[h]

    ya = None
    scores = qk(*heads[0]) if mix_to is not None else None
    for idx, (c, h) in enumerate(heads):
        if mix_to is not None:
            nxt = qk(*heads[idx + 1]) if idx + 1 < len(heads) else None
            spatial_gate(c, h)
            retention(c, h, scores)
            scores = nxt
        if proj_from is not None:
            cols = slice((idx // 2) * R_V_DIM, (idx // 2 + 1) * R_V_DIM)
            if idx % 2 == 0:
                ya = jnp.dot(proj_from[0][...], woa_ref[:, cols], preferred_element_type=F32)
            else:
                yb = jnp.dot(proj_from[1][...], wob_ref[:, cols], preferred_element_type=F32)
                merged = ga_ref[:, cols].astype(F32) * ya + gb_ref[:, cols].astype(F32) * yb
                m_ref[:, cols] = merged.astype(BF16)


def _mixproj_kernel(chunk_decay, blocks_per_seq,
                    u_ref, vn_ref, az_ref, qk_ref, v_ref, rz_ref, ga_ref, gb_ref,
                    ws_ref, bsb_ref, woa_ref, wob_ref, m_ref,
                    state, a0, b0, a1, b1):
    t = pl.program_id(0)
    n_blocks = pl.num_programs(0) - 1

    @pl.when(t % blocks_per_seq == 0)
    def _():
        state[...] = jnp.zeros_like(state)

    step = functools.partial(_mix_and_proj, chunk_decay, u_ref, vn_ref, az_ref, qk_ref, v_ref, rz_ref,
                             ga_ref, gb_ref, ws_ref, bsb_ref, woa_ref, wob_ref, m_ref, state)

    @pl.when(t == 0)
    def _():
        step((a0, b0), None)

    @pl.when((t > 0) & (t < n_blocks) & (t % 2 == 1))
    def _():
        step((a1, b1), (a0, b0))

    @pl.when((t > 0) & (t < n_blocks) & (t % 2 == 0))
    def _():
        step((a0, b0), (a1, b1))

    @pl.when(t == n_blocks)
    def _():
        step(None, (a1, b1))


def _mixproj(p, ws, bsb, woa, wob, chunk_decay, seq):
    tokens = p.shape[0]
    tb = MIX_ROWS
    n_blocks = tokens // tb
    assert n_blocks % 2 == 0
    last = n_blocks - 1

    def cur(jcol):
        return pl.BlockSpec((tb, D_MODEL), lambda t: (jnp.minimum(t, last), jcol))

    def prev(jcol):
        return pl.BlockSpec((tb, D_MODEL), lambda t: (jnp.maximum(t - 1, 0), jcol))

    def whole(a):
        nd = a.ndim
        return pl.BlockSpec(a.shape, lambda t: (0,) * nd)

    wspec = pl.BlockSpec((D_MODEL, D_MODEL), lambda t: (0, 0), pipeline_mode=pl.Buffered(1))
    blk = tb * D_MODEL
    vmem = (2 * D_MODEL * D_MODEL * 2 + 2 * 8 * blk * 2 + 2 * blk * 2 + 4 * blk * 2
            + R_HEADS * R_QK_DIM * R_V_DIM * 4 + 4 * blk * 4 + (4 << 20))
    return pl.pallas_call(
        functools.partial(_mixproj_kernel, chunk_decay, seq // tb),
        grid=(n_blocks + 1,),
        in_specs=[cur(0), cur(1), cur(2), cur(3), cur(4), cur(5), prev(6), prev(7),
                  whole(ws), whole(bsb), wspec, wspec],
        out_specs=pl.BlockSpec((tb, D_MODEL), lambda t: (jnp.maximum(t - 1, 0), 0)),
        out_shape=jax.ShapeDtypeStruct((tokens, D_MODEL), BF16),
        scratch_shapes=[pltpu.VMEM((R_HEADS, R_QK_DIM, R_V_DIM), F32)]
        + [pltpu.VMEM((tb, D_MODEL), BF16)] * 4,
        compiler_params=pltpu.CompilerParams(
            dimension_semantics=("arbitrary",),
            vmem_limit_bytes=min(vmem + (6 << 20), VMEM_CEILING)),
        name="mixproj",
    )(p, p, p, p, p, p, p, p, ws, bsb, woa, wob)


def _out_kernel(m_ref, x_ref, wout_ref, lng_ref, lnb_ref, o_ref):
    for q in range(m_ref.shape[0] // SLAB_ROWS):
        rows = slice(q * SLAB_ROWS, (q + 1) * SLAB_ROWS)
        out = jnp.dot(m_ref[rows, :], wout_ref[...], preferred_element_type=F32)
        y = _standardize_rows(np.float32(DEEPNORM_ALPHA) * x_ref[rows, :] + out)
        o_ref[rows, :] = y * lng_ref[...] + lnb_ref[...]


def _finish(m, x2, wout, ln_g, ln_b):
    tokens = x2.shape[0]
    tb = OUT_ROWS
    row = lambda i: (i, 0)
    const = lambda i: (0, 0)
    blk = tb * D_MODEL
    vmem = D_MODEL * D_MODEL * 2 + 2 * (blk * 2 + 2 * blk * 4) + 4 * SLAB_ROWS * D_MODEL * 4
    return pl.pallas_call(
        _out_kernel,
        grid=(tokens // tb,),
        in_specs=[
            pl.BlockSpec((tb, D_MODEL), row),
            pl.BlockSpec((tb, D_MODEL), row),
            pl.BlockSpec((D_MODEL, D_MODEL), const, pipeline_mode=pl.Buffered(1)),
            pl.BlockSpec((1, D_MODEL), const),
            pl.BlockSpec((1, D_MODEL), const),
        ],
        out_specs=pl.BlockSpec((tb, D_MODEL), row),
        out_shape=jax.ShapeDtypeStruct((tokens, D_MODEL), F32),
        compiler_params=pltpu.CompilerParams(
            dimension_semantics=("arbitrary",),
            vmem_limit_bytes=min(vmem + (6 << 20), VMEM_CEILING)),
        name="finish",
    )(m, x2, wout, ln_g, ln_b)


def _rotary_tables(seq):
    half = R_QK_DIM // 2
    freqs = ROPE_BASE ** (-jnp.arange(0, R_QK_DIM, 2, dtype=F32) / R_QK_DIM)
    ang = jnp.arange(seq, dtype=jnp.int32).astype(F32)[:, None] * freqs[None, :]
    cos, sin = jnp.cos(ang), jnp.sin(ang)
    assert cos.shape == (seq, half)
    return jnp.concatenate([cos, cos], axis=-1), jnp.concatenate([-sin, sin], axis=-1)


def _retention_constants():
    log_gamma = np.log1p(-np.exp2(-5.0 - np.arange(R_HEADS, dtype=np.float64)))
    pos = np.arange(CHUNK, dtype=np.float64) + 1.0
    q_fac = np.exp(log_gamma[None, :] * pos[:, None])
    k_fac = np.exp(-log_gamma[None, :] * pos[:, None]) * R_QK_DIM ** -0.5
    table = np.concatenate([np.repeat(q_fac, R_QK_DIM, axis=1), np.repeat(k_fac, R_QK_DIM, axis=1)], axis=1)
    chunk_decay = tuple(float(np.float32(v)) for v in np.exp(log_gamma * CHUNK))
    return jnp.asarray(table, F32), chunk_decay


def kernel(x, w_in, b_gate, ln_v_g, ln_v_b, w_s, b_s, w_oa, w_ob, w_out, ln_g, ln_b):
    batch, seq, d = x.shape
    assert d == D_MODEL and w_in.shape == (DEPTH, D_MODEL, IN_WIDTH)
    assert seq % PROJ_ROWS == 0 and seq % MIX_ROWS == 0 and (batch * seq) % OUT_ROWS == 0
    tokens = batch * seq
    x2 = x.reshape(tokens, D_MODEL)
    cos_t, sin_t = _rotary_tables(seq)
    dk_t, chunk_decay = _retention_constants()
    for l in range(DEPTH):
        p = _project(x2, w_in[l], ln_v_g[l][None, :], ln_v_b[l][None, :],
                     b_gate[l].reshape(2, D_MODEL), cos_t, sin_t, dk_t, seq)
        bsb = jnp.broadcast_to(b_s[l][:, :, None], (A_GROUPS, CHUNK, CHUNK))
        m = _mixproj(p, w_s[l], bsb, w_oa[l].astype(BF16), w_ob[l].astype(BF16), chunk_decay, seq)
        x2 = _finish(m, x2, w_out[l].astype(BF16), ln_g[l][None, :], ln_b[l][None, :])
    return x2.reshape(batch, seq, D_MODEL)
```

```python
import functools

import numpy as np
import jax
import jax.numpy as jnp
from jax import lax
from jax.experimental import pallas as pl
from jax.experimental.pallas import tpu as pltpu

D_MODEL = 2048
CHUNK = 128
A_GROUPS = 8
A_GROUP_DIM = D_MODEL // A_GROUPS
R_HEADS = 8
R_QK_DIM = D_MODEL // (2 * R_HEADS)
R_V_DIM = D_MODEL // R_HEADS
R_QK_WIDTH = R_HEADS * R_QK_DIM
ROPE_BASE = 10000.0
LN_EPS = 1e-5
DEPTH = 1
DEEPNORM_ALPHA = (2 * DEPTH) ** 0.25
N_COL_GROUPS = 8
IN_WIDTH = N_COL_GROUPS * D_MODEL

V7X_VMEM_BYTES = 64 * 1024 * 1024
SPILL_ALLOWANCE_BYTES = 6 << 20
VMEM_CEILING = V7X_VMEM_BYTES - SPILL_ALLOWANCE_BYTES

PROJ_ROWS = 1024
LOOP_ROWS = 512
W_SLABS = 8
MIX_ROWS = 256
OUT_ROWS = 1024
SLAB_ROWS = 256

F32 = jnp.float32
BF16 = jnp.bfloat16


def _gelu_tanh(x):
    c = np.float32(np.sqrt(2.0 / np.pi))
    ca = np.float32(np.sqrt(2.0 / np.pi) * 0.044715)
    hx = 0.5 * x
    return hx + hx * jnp.tanh(x * (c + ca * (x * x)))


def _sigmoid(x):
    return 0.5 * (jnp.tanh(0.5 * x) + 1.0)


def _silu(x):
    hx = 0.5 * x
    return hx + hx * jnp.tanh(hx)


def _standardize_rows(x):
    mu = jnp.mean(x, axis=-1, keepdims=True)
    d = x - mu
    var = jnp.mean(d * d, axis=-1, keepdims=True)
    return d * lax.rsqrt(var + LN_EPS)


def _proj_kernel(n_w, x_ref, w_ref, lng_ref, lnb_ref, bg_ref, cos_ref, sin_ref, dk_ref, o_ref, wbuf):
    jj = pl.program_id(0)
    i = pl.program_id(1)
    j = jj - 1
    w_rows = w_ref.shape[0]

    @pl.when((jj < N_COL_GROUPS) & (i < n_w))
    def _():
        dst = pl.ds(pl.multiple_of(i * w_rows, w_rows), w_rows)
        wbuf[jj % 2, dst, :] = w_ref[...].astype(BF16)

    def per_slab(epilogue):
        slot = j % 2

        def body(part, carry):
            for q in range(LOOP_ROWS // SLAB_ROWS):
                start = pl.multiple_of(part * LOOP_ROWS + q * SLAB_ROWS, SLAB_ROWS)
                lhs = x_ref[pl.ds(start, SLAB_ROWS), :].astype(BF16)
                epilogue(jnp.dot(lhs, wbuf[slot], preferred_element_type=F32), start)
            return carry
        lax.fori_loop(0, x_ref.shape[0] // LOOP_ROWS, body, 0)

    def store(fn):
        def epilogue(acc, start):
            o_ref[pl.ds(start, SLAB_ROWS), :] = fn(acc).astype(BF16)
        return epilogue

    @pl.when((j == 0))
    def _():
        per_slab(store(_gelu_tanh))

    @pl.when(j == 1)
    def _():
        per_slab(store(lambda acc: _standardize_rows(_gelu_tanh(acc)) * lng_ref[...] + lnb_ref[...]))

    @pl.when((j == 2) | (j == 5))
    def _():
        per_slab(store(_silu))

    @pl.when(j == 3)
    def _():
        def epilogue(acc, start):
            cos = cos_ref[pl.ds(start, SLAB_ROWS), :]
            sin = sin_ref[pl.ds(start, SLAB_ROWS), :]
            for c in range(2 * R_HEADS):
                cols = slice(c * R_QK_DIM, (c + 1) * R_QK_DIM)
                blk = acc[:, cols]
                rot = blk * cos + pltpu.roll(blk, R_QK_DIM // 2, 1) * sin
                for rr in range(SLAB_ROWS // CHUNK):
                    sub = slice(rr * CHUNK, (rr + 1) * CHUNK)
                    dst = pl.ds(pl.multiple_of(start + rr * CHUNK, CHUNK), CHUNK)
                    o_ref[dst, cols] = (rot[sub] * dk_ref[:, cols]).astype(BF16)
        per_slab(epilogue)

    @pl.when(j == 4)
    def _():
        per_slab(store(lambda acc: acc))

    @pl.when(j >= 6)
    def _():
        bias = bg_ref[pl.ds(j - 6, 1), :]
        per_slab(store(lambda acc: _sigmoid(acc + bias)))


def _project(x2, w, ln_v_g, ln_v_b, b_gate2, cos_t, sin_t, dk_t, seq):
    tokens = x2.shape[0]
    bm = PROJ_ROWS
    n_i = tokens // bm
    pos_blocks = seq // bm
    n_w = min(n_i, W_SLABS)
    w_rows = D_MODEL // n_w
    last_group = N_COL_GROUPS - 1
    vmem = (2 * (bm * D_MODEL * 4 + w_rows * D_MODEL * 4 + bm * D_MODEL * 2)
            + 2 * D_MODEL * D_MODEL * 2 + 4 * SLAB_ROWS * D_MODEL * 4)
    const = lambda jj, i: (0, 0)
    rows = lambda jj, i: jnp.where(jj == 0, 0, i)
    w_slab = lambda jj, i: jnp.where(jj > last_group, n_w - 1, jnp.minimum(i, n_w - 1))
    return pl.pallas_call(
        functools.partial(_proj_kernel, n_w),
        grid=(N_COL_GROUPS + 1, n_i),
        in_specs=[
            pl.BlockSpec((bm, D_MODEL), lambda jj, i: (rows(jj, i), 0)),
            pl.BlockSpec((w_rows, D_MODEL), lambda jj, i: (w_slab(jj, i), jnp.minimum(jj, last_group))),
            pl.BlockSpec((1, D_MODEL), const),
            pl.BlockSpec((1, D_MODEL), const),
            pl.BlockSpec((2, D_MODEL), const),
            pl.BlockSpec((bm, R_QK_DIM), lambda jj, i: (i % pos_blocks, 0)),
            pl.BlockSpec((bm, R_QK_DIM), lambda jj, i: (i % pos_blocks, 0)),
            pl.BlockSpec((CHUNK, D_MODEL), const),
        ],
        out_specs=pl.BlockSpec((bm, D_MODEL), lambda jj, i: (rows(jj, i), jnp.maximum(jj - 1, 0))),
        out_shape=jax.ShapeDtypeStruct((tokens, IN_WIDTH), BF16),
        scratch_shapes=[pltpu.VMEM((2, D_MODEL, D_MODEL), BF16)],
        compiler_params=pltpu.CompilerParams(
            dimension_semantics=("arbitrary", "arbitrary"),
            vmem_limit_bytes=min(vmem + SPILL_ALLOWANCE_BYTES, VMEM_CEILING)),
        name="proj",
    )(x2, w, ln_v_g, ln_v_b, b_gate2, cos_t, sin_t, dk_t)


def _mix_and_proj(chunk_decay, u_ref, vn_ref, az_ref, qk_ref, v_ref, rz_ref, ga_ref, gb_ref,
                  ws_ref, bsb_ref, woa_ref, wob_ref, m_ref, state, mix_to, proj_from):
    rows = u_ref.shape[0]
    heads = [(c, h) for c in range(rows // CHUNK) for h in range(R_HEADS)]
    n_tiles = D_MODEL // R_V_DIM
    assert len(heads) == 2 * n_tiles
    if mix_to is not None:
        a_scr, b_scr = mix_to
        t_idx = lax.broadcasted_iota(jnp.int32, (CHUNK, CHUNK), 0)
        s_idx = lax.broadcasted_iota(jnp.int32, (CHUNK, CHUNK), 1)
        causal = t_idx >= s_idx
        ws_causal = [jnp.where(causal, ws_ref[g], 0.0).astype(BF16) for g in range(A_GROUPS)]

    def qk(c, h):
        r = slice(c * CHUNK, (c + 1) * CHUNK)
        q = qk_ref[r, h * R_QK_DIM:(h + 1) * R_QK_DIM]
        k = qk_ref[r, R_QK_WIDTH + h * R_QK_DIM:R_QK_WIDTH + (h + 1) * R_QK_DIM]
        return lax.dot_general(q, k, (((1,), (1,)), ((), ())), preferred_element_type=F32)

    def spatial_gate(c, g):
        r = slice(c * CHUNK, (c + 1) * CHUNK)
        cs = slice(g * A_GROUP_DIM, (g + 1) * A_GROUP_DIM)
        sv = jnp.dot(ws_causal[g], vn_ref[r, cs], preferred_element_type=F32)
        bias = bsb_ref[g]
        sv = jnp.concatenate([sv[:, :CHUNK] + bias, sv[:, CHUNK:] + bias], axis=1)
        a = u_ref[r, cs].astype(F32) * sv * az_ref[r, cs].astype(F32)
        a_scr[r, cs] = a.astype(BF16)

    def retention(c, h, s):
        r = slice(c * CHUNK, (c + 1) * CHUNK)
        q = qk_ref[r, h * R_QK_DIM:(h + 1) * R_QK_DIM]
        k = qk_ref[r, R_QK_WIDTH + h * R_QK_DIM:R_QK_WIDTH + (h + 1) * R_QK_DIM]
        vs = slice(h * R_V_DIM, (h + 1) * R_V_DIM)
        v = v_ref[r, vs]
        s = jnp.where(causal, s, 0.0).astype(BF16)
        st = state[h]
        lhs = jnp.concatenate([s, q], axis=1)
        rhs = jnp.concatenate([v, st.astype(BF16)], axis=0)
        o = _standardize_rows(jnp.dot(lhs, rhs, preferred_element_type=F32))
        b_scr[r, vs] = (o * rz_ref[r, vs].astype(F32)).astype(BF16)
        kv = lax.dot_general(k, v, (((0,), (0,)), ((), ())), preferred_element_type=F32)
        state[h] = (st + kv) * chunk_decay[h]

    ya = None
    scores = qk(*heads[0]) if mix_to is not None else None
    for idx, (c, h) in enumerate(heads):
        if mix_to is not None:
            nxt = qk(*heads[idx + 1]) if idx + 1 < len(heads) else None
            spatial_gate(c, h)
            retention(c, h, scores)
            scores = nxt
        if proj_from is not None:
            cols = slice((idx // 2) * R_V_DIM, (idx // 2 + 1) * R_V_DIM)
            if idx % 2 == 0:
                ya = jnp.dot(proj_from[0][...], woa_ref[:, cols], preferred_element_type=F32)
            else:
                yb = jnp.dot(proj_from[1][...], wob_ref[:, cols], preferred_element_type=F32)
                merged = ga_ref[:, cols].astype(F32) * ya + gb_ref[:, cols].astype(F32) * yb
                m_ref[:, cols] = merged.astype(BF16)


def _mixproj_kernel(chunk_decay, blocks_per_seq,
                    u_ref, vn_ref, az_ref, qk_ref, v_ref, rz_ref, ga_ref, gb_ref,
                    ws_ref, bsb_ref, woa_ref, wob_ref, m_ref,
                    state, a0, b0, a1, b1):
    t = pl.program_id(0)
    n_blocks = pl.num_programs(0) - 1

    @pl.when(t % blocks_per_seq == 0)
    def _():
        state[...] = jnp.zeros_like(state)

    step = functools.partial(_mix_and_proj, chunk_decay, u_ref, vn_ref, az_ref, qk_ref, v_ref, rz_ref,
                             ga_ref, gb_ref, ws_ref, bsb_ref, woa_ref, wob_ref, m_ref, state)

    @pl.when(t == 0)
    def _():
        step((a0, b0), None)

    @pl.when((t > 0) & (t < n_blocks) & (t % 2 == 1))
    def _():
        step((a1, b1), (a0, b0))

    @pl.when((t > 0) & (t < n_blocks) & (t % 2 == 0))
    def _():
        step((a0, b0), (a1, b1))

    @pl.when(t == n_blocks)
    def _():
        step(None, (a1, b1))


def _mixproj(p, ws, bsb, woa, wob, chunk_decay, seq):
    tokens = p.shape[0]
    tb = MIX_ROWS
    n_blocks = tokens // tb
    assert n_blocks % 2 == 0
    last = n_blocks - 1

    def cur(jcol):
        return pl.BlockSpec((tb, D_MODEL), lambda t: (jnp.minimum(t, last), jcol))

    def prev(jcol):
        return pl.BlockSpec((tb, D_MODEL), lambda t: (jnp.maximum(t - 1, 0), jcol))

    def whole(a):
        nd = a.ndim
        return pl.BlockSpec(a.shape, lambda t: (0,) * nd)

    wspec = pl.BlockSpec((D_MODEL, D_MODEL), lambda t: (0, 0), pipeline_mode=pl.Buffered(1))
    blk = tb * D_MODEL
    vmem = (2 * D_MODEL * D_MODEL * 2 + 2 * 8 * blk * 2 + 2 * blk * 2 + 4 * blk * 2
            + R_HEADS * R_QK_DIM * R_V_DIM * 4 + 4 * blk * 4 + 4 * (ws.size + bsb.size) * 4)
    return pl.pallas_call(
        functools.partial(_mixproj_kernel, chunk_decay, seq // tb),
        grid=(n_blocks + 1,),
        in_specs=[cur(0), cur(1), cur(2), cur(3), cur(4), cur(5), prev(6), prev(7),
                  whole(ws), whole(bsb), wspec, wspec],
        out_specs=pl.BlockSpec((tb, D_MODEL), lambda t: (jnp.maximum(t - 1, 0), 0)),
        out_shape=jax.ShapeDtypeStruct((tokens, D_MODEL), BF16),
        scratch_shapes=[pltpu.VMEM((R_HEADS, R_QK_DIM, R_V_DIM), F32)]
        + [pltpu.VMEM((tb, D_MODEL), BF16)] * 4,
        compiler_params=pltpu.CompilerParams(
            dimension_semantics=("arbitrary",),
            vmem_limit_bytes=min(vmem + SPILL_ALLOWANCE_BYTES, VMEM_CEILING)),
        name="mixproj",
    )(p, p, p, p, p, p, p, p, ws, bsb, woa, wob)


def _out_kernel(m_ref, x_ref, wout_ref, lng_ref, lnb_ref, o_ref):
    for q in range(m_ref.shape[0] // SLAB_ROWS):
        rows = slice(q * SLAB_ROWS, (q + 1) * SLAB_ROWS)
        out = jnp.dot(m_ref[rows, :], wout_ref[...], preferred_element_type=F32)
        y = _standardize_rows(np.float32(DEEPNORM_ALPHA) * x_ref[rows, :] + out)
        o_ref[rows, :] = y * lng_ref[...] + lnb_ref[...]


def _finish(m, x2, wout, ln_g, ln_b):
    tokens = x2.shape[0]
    tb = OUT_ROWS
    row = lambda i: (i, 0)
    const = lambda i: (0, 0)
    blk = tb * D_MODEL
    vmem = D_MODEL * D_MODEL * 2 + 2 * (blk * 2 + 2 * blk * 4) + 4 * SLAB_ROWS * D_MODEL * 4
    return pl.pallas_call(
        _out_kernel,
        grid=(tokens // tb,),
        in_specs=[
            pl.BlockSpec((tb, D_MODEL), row),
            pl.BlockSpec((tb, D_MODEL), row),
            pl.BlockSpec((D_MODEL, D_MODEL), const, pipeline_mode=pl.Buffered(1)),
            pl.BlockSpec((1, D_MODEL), const),
            pl.BlockSpec((1, D_MODEL), const),
        ],
        out_specs=pl.BlockSpec((tb, D_MODEL), row),
        out_shape=jax.ShapeDtypeStruct((tokens, D_MODEL), F32),
        compiler_params=pltpu.CompilerParams(
            dimension_semantics=("arbitrary",),
            vmem_limit_bytes=min(vmem + SPILL_ALLOWANCE_BYTES, VMEM_CEILING)),
        name="finish",
    )(m, x2, wout, ln_g, ln_b)


def _rotary_tables(seq):
    half = R_QK_DIM // 2
    freqs = ROPE_BASE ** (-jnp.arange(0, R_QK_DIM, 2, dtype=F32) / R_QK_DIM)
    ang = jnp.arange(seq, dtype=jnp.int32).astype(F32)[:, None] * freqs[None, :]
    cos, sin = jnp.cos(ang), jnp.sin(ang)
    assert cos.shape == (seq, half)
    return jnp.concatenate([cos, cos], axis=-1), jnp.concatenate([-sin, sin], axis=-1)


def _retention_constants():
    log_gamma = np.log1p(-np.exp2(-5.0 - np.arange(R_HEADS, dtype=np.float64)))
    pos = np.arange(CHUNK, dtype=np.float64) + 1.0
    q_fac = np.exp(log_gamma[None, :] * pos[:, None])
    k_fac = np.exp(-log_gamma[None, :] * pos[:, None]) * R_QK_DIM ** -0.5
    table = np.concatenate([np.repeat(q_fac, R_QK_DIM, axis=1), np.repeat(k_fac, R_QK_DIM, axis=1)], axis=1)
    chunk_decay = tuple(float(np.float32(v)) for v in np.exp(log_gamma * CHUNK))
    return jnp.asarray(table, F32), chunk_decay


def kernel(x, w_in, b_gate, ln_v_g, ln_v_b, w_s, b_s, w_oa, w_ob, w_out, ln_g, ln_b):
    batch, seq, d = x.shape
    assert d == D_MODEL and w_in.shape == (DEPTH, D_MODEL, IN_WIDTH)
    assert seq % PROJ_ROWS == 0 and seq % MIX_ROWS == 0 and (batch * seq) % OUT_ROWS == 0
    tokens = batch * seq
    x2 = x.reshape(tokens, D_MODEL)
    cos_t, sin_t = _rotary_tables(seq)
    dk_t, chunk_decay = _retention_constants()
    for l in range(DEPTH):
        p = _project(x2, w_in[l], ln_v_g[l][None, :], ln_v_b[l][None, :],
                     b_gate[l].reshape(2, D_MODEL), cos_t, sin_t, dk_t, seq)
        bsb = jnp.broadcast_to(b_s[l][:, :, None], (A_GROUPS, CHUNK, CHUNK))
        m = _mixproj(p, w_s[l], bsb, w_oa[l].astype(BF16), w_ob[l].astype(BF16), chunk_decay, seq)
        x2 = _finish(m, x2, w_out[l].astype(BF16), ln_g[l][None, :], ln_b[l][None, :])
    return x2.reshape(batch, seq, D_MODEL)
```

```python
import functools

import numpy as np
import jax
import jax.numpy as jnp
from jax import lax
from jax.experimental import pallas as pl
from jax.experimental.pallas import tpu as pltpu

D_MODEL = 2048
CHUNK = 128
A_GROUPS = 8
A_GROUP_DIM = D_MODEL // A_GROUPS
R_HEADS = 8
R_QK_DIM = D_MODEL // (2 * R_HEADS)
R_V_DIM = D_MODEL // R_HEADS
R_QK_WIDTH = R_HEADS * R_QK_DIM
ROPE_BASE = 10000.0
LN_EPS = 1e-5
DEPTH = 1
DEEPNORM_ALPHA = (2 * DEPTH) ** 0.25
N_COL_GROUPS = 8
IN_WIDTH = N_COL_GROUPS * D_MODEL

V7X_VMEM_BYTES = 64 * 1024 * 1024
SPILL_ALLOWANCE_BYTES = 6 << 20
VMEM_CEILING = V7X_VMEM_BYTES - SPILL_ALLOWANCE_BYTES

PROJ_ROWS = 1024
LOOP_ROWS = 512
W_SLABS = 8
WO_STEPS = 128
MIX_ROWS = 256
OUT_ROWS = 1024
SLAB_ROWS = 256

F32 = jnp.float32
BF16 = jnp.bfloat16


def _gelu_tanh(x):
    c = np.float32(np.sqrt(2.0 / np.pi))
    ca = np.float32(np.sqrt(2.0 / np.pi) * 0.044715)
    hx = 0.5 * x
    return hx + hx * jnp.tanh(x * (c + ca * (x * x)))


def _sigmoid(x):
    return 0.5 * (jnp.tanh(0.5 * x) + 1.0)


def _silu(x):
    hx = 0.5 * x
    return hx + hx * jnp.tanh(hx)


def _standardize_rows(x):
    mu = jnp.mean(x, axis=-1, keepdims=True)
    d = x - mu
    var = jnp.mean(d * d, axis=-1, keepdims=True)
    return d * lax.rsqrt(var + LN_EPS)


def _proj_kernel(n_w, n_wo, x_ref, w_ref, lng_ref, lnb_ref, bg_ref, cos_ref, sin_ref, dk_ref,
                 woa_ref, wob_ref, wout_ref, o_ref, woa_o, wob_o, wout_o, wbuf):
    jj = pl.program_id(0)
    i = pl.program_id(1)
    j = jj - 1
    w_rows = w_ref.shape[0]

    @pl.when(jj * pl.num_programs(1) + i < n_wo)
    def _():
        woa_o[...] = woa_ref[...].astype(BF16)
        wob_o[...] = wob_ref[...].astype(BF16)
        wout_o[...] = wout_ref[...].astype(BF16)

    @pl.when((jj < N_COL_GROUPS) & (i < n_w))
    def _():
        dst = pl.ds(pl.multiple_of(i * w_rows, w_rows), w_rows)
        wbuf[jj % 2, dst, :] = w_ref[...].astype(BF16)

    def per_slab(epilogue):
        slot = j % 2

        def body(part, carry):
            for q in range(LOOP_ROWS // SLAB_ROWS):
                start = pl.multiple_of(part * LOOP_ROWS + q * SLAB_ROWS, SLAB_ROWS)
                lhs = x_ref[pl.ds(start, SLAB_ROWS), :].astype(BF16)
                epilogue(jnp.dot(lhs, wbuf[slot], preferred_element_type=F32), start)
            return carry
        lax.fori_loop(0, x_ref.shape[0] // LOOP_ROWS, body, 0)

    def store(fn):
        def epilogue(acc, start):
            o_ref[pl.ds(start, SLAB_ROWS), :] = fn(acc).astype(BF16)
        return epilogue

    @pl.when((j == 0))
    def _():
        per_slab(store(_gelu_tanh))

    @pl.when(j == 1)
    def _():
        per_slab(store(lambda acc: _standardize_rows(_gelu_tanh(acc)) * lng_ref[...] + lnb_ref[...]))

    @pl.when((j == 2) | (j == 5))
    def _():
        per_slab(store(_silu))

    @pl.when(j == 3)
    def _():
        def epilogue(acc, start):
            cos = cos_ref[pl.ds(start, SLAB_ROWS), :]
            sin = sin_ref[pl.ds(start, SLAB_ROWS), :]
            for c in range(2 * R_HEADS):
                cols = slice(c * R_QK_DIM, (c + 1) * R_QK_DIM)
                blk = acc[:, cols]
                rot = blk * cos + pltpu.roll(blk, R_QK_DIM // 2, 1) * sin
                for rr in range(SLAB_ROWS // CHUNK):
                    sub = slice(rr * CHUNK, (rr + 1) * CHUNK)
                    dst = pl.ds(pl.multiple_of(start + rr * CHUNK, CHUNK), CHUNK)
                    o_ref[dst, cols] = (rot[sub] * dk_ref[:, cols]).astype(BF16)
        per_slab(epilogue)

    @pl.when(j == 4)
    def _():
        per_slab(store(lambda acc: acc))

    @pl.when(j >= 6)
    def _():
        bias = bg_ref[pl.ds(j - 6, 1), :]
        per_slab(store(lambda acc: _sigmoid(acc + bias)))


def _project(x2, w, ln_v_g, ln_v_b, b_gate2, cos_t, sin_t, dk_t, w_oa, w_ob, w_out, seq):
    tokens = x2.shape[0]
    bm = PROJ_ROWS
    n_i = tokens // bm
    pos_blocks = seq // bm
    n_w = min(n_i, W_SLABS)
    w_rows = D_MODEL // n_w
    last_group = N_COL_GROUPS - 1
    n_wo = WO_STEPS
    while n_wo > (N_COL_GROUPS + 1) * n_i:
        n_wo //= 2
    wo_rows = D_MODEL // n_wo
    vmem = (2 * (bm * D_MODEL * 4 + w_rows * D_MODEL * 4 + bm * D_MODEL * 2)
            + 2 * D_MODEL * D_MODEL * 2 + 4 * SLAB_ROWS * D_MODEL * 4 + 2 * 3 * wo_rows * D_MODEL * 6)
    const = lambda jj, i: (0, 0)
    rows = lambda jj, i: jnp.where(jj == 0, 0, i)
    w_slab = lambda jj, i: jnp.where(jj > last_group, n_w - 1, jnp.minimum(i, n_w - 1))
    wo_spec = pl.BlockSpec((wo_rows, D_MODEL), lambda jj, i: (jnp.minimum(jj * n_i + i, n_wo - 1), 0))
    wo_shape = jax.ShapeDtypeStruct((D_MODEL, D_MODEL), BF16)
    return pl.pallas_call(
        functools.partial(_proj_kernel, n_w, n_wo),
        grid=(N_COL_GROUPS + 1, n_i),
        in_specs=[
            pl.BlockSpec((bm, D_MODEL), lambda jj, i: (rows(jj, i), 0)),
            pl.BlockSpec((w_rows, D_MODEL), lambda jj, i: (w_slab(jj, i), jnp.minimum(jj, last_group))),
            pl.BlockSpec((1, D_MODEL), const),
            pl.BlockSpec((1, D_MODEL), const),
            pl.BlockSpec((2, D_MODEL), const),
            pl.BlockSpec((bm, R_QK_DIM), lambda jj, i: (i % pos_blocks, 0)),
            pl.BlockSpec((bm, R_QK_DIM), lambda jj, i: (i % pos_blocks, 0)),
            pl.BlockSpec((CHUNK, D_MODEL), const),
            wo_spec, wo_spec, wo_spec,
        ],
        out_specs=[pl.BlockSpec((bm, D_MODEL), lambda jj, i: (rows(jj, i), jnp.maximum(jj - 1, 0))),
                   wo_spec, wo_spec, wo_spec],
        out_shape=[jax.ShapeDtypeStruct((tokens, IN_WIDTH), BF16), wo_shape, wo_shape, wo_shape],
        scratch_shapes=[pltpu.VMEM((2, D_MODEL, D_MODEL), BF16)],
        compiler_params=pltpu.CompilerParams(
            dimension_semantics=("arbitrary", "arbitrary"),
            vmem_limit_bytes=min(vmem + SPILL_ALLOWANCE_BYTES, VMEM_CEILING)),
        name="proj",
    )(x2, w, ln_v_g, ln_v_b, b_gate2, cos_t, sin_t, dk_t, w_oa, w_ob, w_out)


def _mix_and_proj(chunk_decay, u_ref, vn_ref, az_ref, qk_ref, v_ref, rz_ref, ga_ref, gb_ref,
                  ws_ref, bsb_ref, woa_ref, wob_ref, m_ref, state, mix_to, proj_from):
    rows = u_ref.shape[0]
    heads = [(c, h) for c in range(rows // CHUNK) for h in range(R_HEADS)]
    n_tiles = D_MODEL // R_V_DIM
    assert len(heads) == 2 * n_tiles
    if mix_to is not None:
        a_scr, b_scr = mix_to
        t_idx = lax.broadcasted_iota(jnp.int32, (CHUNK, CHUNK), 0)
        s_idx = lax.broadcasted_iota(jnp.int32, (CHUNK, CHUNK), 1)
        causal = t_idx >= s_idx
        ws_causal = [jnp.where(causal, ws_ref[g], 0.0).astype(BF16) for g in range(A_GROUPS)]

    def qk(c, h):
        r = slice(c * CHUNK, (c + 1) * CHUNK)
        q = qk_ref[r, h * R_QK_DIM:(h + 1) * R_QK_DIM]
        k = qk_ref[r, R_QK_WIDTH + h * R_QK_DIM:R_QK_WIDTH + (h + 1) * R_QK_DIM]
        return lax.dot_general(q, k, (((1,), (1,)), ((), ())), preferred_element_type=F32)

    def spatial_gate(c, g):
        r = slice(c * CHUNK, (c + 1) * CHUNK)
        cs = slice(g * A_GROUP_DIM, (g + 1) * A_GROUP_DIM)
        sv = jnp.dot(ws_causal[g], vn_ref[r, cs], preferred_element_type=F32)
        bias = bsb_ref[g]
        sv = jnp.concatenate([sv[:, :CHUNK] + bias, sv[:, CHUNK:] + bias], axis=1)
        a = u_ref[r, cs].astype(F32) * sv * az_ref[r, cs].astype(F32)
        a_scr[r, cs] = a.astype(BF16)

    def retention(c, h, s):
        r = slice(c * CHUNK, (c + 1) * CHUNK)
        q = qk_ref[r, h * R_QK_DIM:(h + 1) * R_QK_DIM]
        k = qk_ref[r, R_QK_WIDTH + h * R_QK_DIM:R_QK_WIDTH + (h + 1) * R_QK_DIM]
        vs = slice(h * R_V_DIM, (h + 1) * R_V_DIM)
        v = v_ref[r, vs]
        s = jnp.where(causal, s, 0.0).astype(BF16)
        st = state[h]
        lhs = jnp.concatenate([s, q], axis=1)
        rhs = jnp.concatenate([v, st.astype(BF16)], axis=0)
        o = _standardize_rows(jnp.dot(lhs, rhs, preferred_element_type=F32))
        b_scr[r, vs] = (o * rz_ref[r, vs].astype(F32)).astype(BF16)
        kv = lax.dot_general(k, v, (((0,), (0,)), ((), ())), preferred_element_type=F32)
        state[h] = (st + kv) * chunk_decay[h]

    ya = None
    scores = qk(*heads[0]) if mix_to is not None else None
    for idx, (c, h) in enumerate(heads):
        if mix_to is not None:
            nxt = qk(*heads[idx + 1]) if idx + 1 < len(heads) else None
            spatial_gate(c, h)
            retention(c, h, scores)
            scores = nxt
        if proj_from is not None:
            cols = slice((idx // 2) * R_V_DIM, (idx // 2 + 1) * R_V_DIM)
            if idx % 2 == 0:
                ya = jnp.dot(proj_from[0][...], woa_ref[:, cols], preferred_element_type=F32)
            else:
                yb = jnp.dot(proj_from[1][...], wob_ref[:, cols], preferred_element_type=F32)
                merged = ga_ref[:, cols].astype(F32) * ya + gb_ref[:, cols].astype(F32) * yb
                m_ref[:, cols] = merged.astype(BF16)


def _mixproj_kernel(chunk_decay, blocks_per_seq,
                    u_ref, vn_ref, az_ref, qk_ref, v_ref, rz_ref, ga_ref, gb_ref,
                    ws_ref, bsb_ref, woa_ref, wob_ref, m_ref,
                    state, a0, b0, a1, b1):
    t = pl.program_id(0)
    n_blocks = pl.num_programs(0) - 1

    @pl.when(t % blocks_per_seq == 0)
    def _():
        state[...] = jnp.zeros_like(state)

    step = functools.partial(_mix_and_proj, chunk_decay, u_ref, vn_ref, az_ref, qk_ref, v_ref, rz_ref,
                             ga_ref, gb_ref, ws_ref, bsb_ref, woa_ref, wob_ref, m_ref, state)

    @pl.when(t == 0)
    def _():
        step((a0, b0), None)

    @pl.when((t > 0) & (t < n_blocks) & (t % 2 == 1))
    def _():
        step((a1, b1), (a0, b0))

    @pl.when((t > 0) & (t < n_blocks) & (t % 2 == 0))
    def _():
        step((a0, b0), (a1, b1))

    @pl.when(t == n_blocks)
    def _():
        step(None, (a1, b1))


def _mixproj(p, ws, bsb, woa, wob, chunk_decay, seq):
    tokens = p.shape[0]
    tb = MIX_ROWS
    n_blocks = tokens // tb
    assert n_blocks % 2 == 0
    last = n_blocks - 1

    def cur(jcol):
        return pl.BlockSpec((tb, D_MODEL), lambda t: (jnp.minimum(t, last), jcol))

    def prev(jcol):
        return pl.BlockSpec((tb, D_MODEL), lambda t: (jnp.maximum(t - 1, 0), jcol))

    def whole(a):
        nd = a.ndim
        return pl.BlockSpec(a.shape, lambda t: (0,) * nd)

    wspec = pl.BlockSpec((D_MODEL, D_MODEL), lambda t: (0, 0), pipeline_mode=pl.Buffered(1))
    blk = tb * D_MODEL
    vmem = (2 * D_MODEL * D_MODEL * 2 + 2 * 8 * blk * 2 + 2 * blk * 2 + 4 * blk * 2
            + R_HEADS * R_QK_DIM * R_V_DIM * 4 + 4 * blk * 4 + 4 * (ws.size + bsb.size) * 4)
    return pl.pallas_call(
        functools.partial(_mixproj_kernel, chunk_decay, seq // tb),
        grid=(n_blocks + 1,),
        in_specs=[cur(0), cur(1), cur(2), cur(3), cur(4), cur(5), prev(6), prev(7),
                  whole(ws), whole(bsb), wspec, wspec],
        out_specs=pl.BlockSpec((tb, D_MODEL), lambda t: (jnp.maximum(t - 1, 0), 0)),
        out_shape=jax.ShapeDtypeStruct((tokens, D_MODEL), BF16),
        scratch_shapes=[pltpu.VMEM((R_HEADS, R_QK_DIM, R_V_DIM), F32)]
        + [pltpu.VMEM((tb, D_MODEL), BF16)] * 4,
        compiler_params=pltpu.CompilerParams(
            dimension_semantics=("arbitrary",),
            vmem_limit_bytes=min(vmem + SPILL_ALLOWANCE_BYTES, VMEM_CEILING)),
        name="mixproj",
    )(p, p, p, p, p, p, p, p, ws, bsb, woa, wob)


def _out_kernel(m_ref, x_ref, wout_ref, lng_ref, lnb_ref, o_ref):
    for q in range(m_ref.shape[0] // SLAB_ROWS):
        rows = slice(q * SLAB_ROWS, (q + 1) * SLAB_ROWS)
        out = jnp.dot(m_ref[rows, :], wout_ref[...], preferred_element_type=F32)
        y = _standardize_rows(np.float32(DEEPNORM_ALPHA) * x_ref[rows, :] + out)
        o_ref[rows, :] = y * lng_ref[...] + lnb_ref[...]


def _finish(m, x2, wout, ln_g, ln_b):
    tokens = x2.shape[0]
    tb = OUT_ROWS
    row = lambda i: (i, 0)
    const = lambda i: (0, 0)
    blk = tb * D_MODEL
    vmem = D_MODEL * D_MODEL * 2 + 2 * (blk * 2 + 2 * blk * 4) + 4 * SLAB_ROWS * D_MODEL * 4
    return pl.pallas_call(
        _out_kernel,
        grid=(tokens // tb,),
        in_specs=[
            pl.BlockSpec((tb, D_MODEL), row),
            pl.BlockSpec((tb, D_MODEL), row),
            pl.BlockSpec((D_MODEL, D_MODEL), const, pipeline_mode=pl.Buffered(1)),
            pl.BlockSpec((1, D_MODEL), const),
            pl.BlockSpec((1, D_MODEL), const),
        ],
        out_specs=pl.BlockSpec((tb, D_MODEL), row),
        out_shape=jax.ShapeDtypeStruct((tokens, D_MODEL), F32),
        compiler_params=pltpu.CompilerParams(
            dimension_semantics=("arbitrary",),
            vmem_limit_bytes=min(vmem + SPILL_ALLOWANCE_BYTES, VMEM_CEILING)),
        name="finish",
    )(m, x2, wout, ln_g, ln_b)


def _rotary_tables(seq):
    half = R_QK_DIM // 2
    freqs = ROPE_BASE ** (-jnp.arange(0, R_QK_DIM, 2, dtype=F32) / R_QK_DIM)
    ang = jnp.arange(seq, dtype=jnp.int32).astype(F32)[:, None] * freqs[None, :]
    cos, sin = jnp.cos(ang), jnp.sin(ang)
    assert cos.shape == (seq, half)
    return jnp.concatenate([cos, cos], axis=-1), jnp.concatenate([-sin, sin], axis=-1)


def _retention_constants():
    log_gamma = np.log1p(-np.exp2(-5.0 - np.arange(R_HEADS, dtype=np.float64)))
    pos = np.arange(CHUNK, dtype=np.float64) + 1.0
    q_fac = np.exp(log_gamma[None, :] * pos[:, None])
    k_fac = np.exp(-log_gamma[None, :] * pos[:, None]) * R_QK_DIM ** -0.5
    table = np.concatenate([np.repeat(q_fac, R_QK_DIM, axis=1), np.repeat(k_fac, R_QK_DIM, axis=1)], axis=1)
    chunk_decay = tuple(float(np.float32(v)) for v in np.exp(log_gamma * CHUNK))
    return jnp.asarray(table, F32), chunk_decay


def kernel(x, w_in, b_gate, ln_v_g, ln_v_b, w_s, b_s, w_oa, w_ob, w_out, ln_g, ln_b):
    batch, seq, d = x.shape
    assert d == D_MODEL and w_in.shape == (DEPTH, D_MODEL, IN_WIDTH)
    assert seq % PROJ_ROWS == 0 and seq % MIX_ROWS == 0 and (batch * seq) % OUT_ROWS == 0
    tokens = batch * seq
    x2 = x.reshape(tokens, D_MODEL)
    cos_t, sin_t = _rotary_tables(seq)
    dk_t, chunk_decay = _retention_constants()
    for l in range(DEPTH):
        p, woa, wob, wout = _project(x2, w_in[l], ln_v_g[l][None, :], ln_v_b[l][None, :],
                                     b_gate[l].reshape(2, D_MODEL), cos_t, sin_t, dk_t,
                                     w_oa[l], w_ob[l], w_out[l], seq)
        bsb = jnp.broadcast_to(b_s[l][:, :, None], (A_GROUPS, CHUNK, CHUNK))
        m = _mixproj(p, w_s[l], bsb, woa, wob, chunk_decay, seq)
        x2 = _finish(m, x2, wout, ln_g[l][None, :], ln_b[l][None, :])
    return x2.reshape(batch, seq, D_MODEL)
```

```python
import functools

import numpy as np
import jax
import jax.numpy as jnp
from jax import lax
from jax.experimental import pallas as pl
from jax.experimental.pallas import tpu as pltpu

D_MODEL = 2048
CHUNK = 128
A_GROUPS = 8
A_GROUP_DIM = D_MODEL // A_GROUPS
R_HEADS = 8
R_QK_DIM = D_MODEL // (2 * R_HEADS)
R_V_DIM = D_MODEL // R_HEADS
R_QK_WIDTH = R_HEADS * R_QK_DIM
ROPE_BASE = 10000.0
LN_EPS = 1e-5
DEPTH = 1
DEEPNORM_ALPHA = (2 * DEPTH) ** 0.25
N_COL_GROUPS = 8
IN_WIDTH = N_COL_GROUPS * D_MODEL

V7X_VMEM_BYTES = 64 * 1024 * 1024
SPILL_ALLOWANCE_BYTES = 6 << 20
VMEM_CEILING = V7X_VMEM_BYTES - SPILL_ALLOWANCE_BYTES

PROJ_ROWS = 1024
LOOP_ROWS = 512
W_SLABS = 8
WO_STEPS = 128
MIX_ROWS = 256
OUT_ROWS = 1024
SLAB_ROWS = 256

F32 = jnp.float32
BF16 = jnp.bfloat16


def _gelu_tanh(x):
    c = np.float32(np.sqrt(2.0 / np.pi))
    ca = np.float32(np.sqrt(2.0 / np.pi) * 0.044715)
    hx = 0.5 * x
    return hx + hx * jnp.tanh(x * (c + ca * (x * x)))


def _sigmoid(x):
    return 0.5 * (jnp.tanh(0.5 * x) + 1.0)


def _silu(x):
    hx = 0.5 * x
    return hx + hx * jnp.tanh(hx)


def _standardize_rows(x):
    mu = jnp.mean(x, axis=-1, keepdims=True)
    d = x - mu
    var = jnp.mean(d * d, axis=-1, keepdims=True)
    return d * lax.rsqrt(var + LN_EPS)


def _proj_kernel(n_w, n_wo, x_ref, w_ref, lng_ref, lnb_ref, bg_ref, cos_ref, sin_ref, dk_ref,
                 woa_ref, wob_ref, wout_ref, o_ref, woa_o, wob_o, wout_o, wbuf):
    jj = pl.program_id(0)
    i = pl.program_id(1)
    j = jj - 1
    w_rows = w_ref.shape[0]

    @pl.when(jj * pl.num_programs(1) + i < n_wo)
    def _():
        woa_o[...] = woa_ref[...].astype(BF16)
        wob_o[...] = wob_ref[...].astype(BF16)
        wout_o[...] = wout_ref[...].astype(BF16)

    @pl.when((jj < N_COL_GROUPS) & (i < n_w))
    def _():
        dst = pl.ds(pl.multiple_of(i * w_rows, w_rows), w_rows)
        wbuf[jj % 2, dst, :] = w_ref[...].astype(BF16)

    def per_slab(epilogue):
        slot = j % 2

        def body(part, carry):
            for q in range(LOOP_ROWS // SLAB_ROWS):
                start = pl.multiple_of(part * LOOP_ROWS + q * SLAB_ROWS, SLAB_ROWS)
                lhs = x_ref[pl.ds(start, SLAB_ROWS), :].astype(BF16)
                epilogue(jnp.dot(lhs, wbuf[slot], preferred_element_type=F32), start)
            return carry
        lax.fori_loop(0, x_ref.shape[0] // LOOP_ROWS, body, 0)

    def store(fn):
        def epilogue(acc, start):
            o_ref[pl.ds(start, SLAB_ROWS), :] = fn(acc).astype(BF16)
        return epilogue

    @pl.when((j == 0))
    def _():
        per_slab(store(_gelu_tanh))

    @pl.when(j == 1)
    def _():
        per_slab(store(lambda acc: _standardize_rows(_gelu_tanh(acc)) * lng_ref[...] + lnb_ref[...]))

    @pl.when((j == 2) | (j == 5))
    def _():
        per_slab(store(_silu))

    @pl.when(j == 3)
    def _():
        def epilogue(acc, start):
            cos = cos_ref[pl.ds(start, SLAB_ROWS), :]
            sin = sin_ref[pl.ds(start, SLAB_ROWS), :]
            for c in range(2 * R_HEADS):
                cols = slice(c * R_QK_DIM, (c + 1) * R_QK_DIM)
                blk = acc[:, cols]
                rot = blk * cos + pltpu.roll(blk, R_QK_DIM // 2, 1) * sin
                for rr in range(SLAB_ROWS // CHUNK):
                    sub = slice(rr * CHUNK, (rr + 1) * CHUNK)
                    dst = pl.ds(pl.multiple_of(start + rr * CHUNK, CHUNK), CHUNK)
                    o_ref[dst, cols] = (rot[sub] * dk_ref[:, cols]).astype(BF16)
        per_slab(epilogue)

    @pl.when(j == 4)
    def _():
        per_slab(store(lambda acc: acc))

    @pl.when(j >= 6)
    def _():
        bias = bg_ref[pl.ds(j - 6, 1), :]
        per_slab(store(lambda acc: _sigmoid(acc + bias)))


def _project(x2, w, ln_v_g, ln_v_b, b_gate2, cos_t, sin_t, dk_t, w_oa, w_ob, w_out, seq):
    tokens = x2.shape[0]
    bm = PROJ_ROWS
    n_i = tokens // bm
    pos_blocks = seq // bm
    n_w = min(n_i, W_SLABS)
    w_rows = D_MODEL // n_w
    last_group = N_COL_GROUPS - 1
    n_wo = WO_STEPS
    while n_wo > (N_COL_GROUPS + 1) * n_i:
        n_wo //= 2
    wo_rows = D_MODEL // n_wo
    vmem = (2 * (bm * D_MODEL * 4 + w_rows * D_MODEL * 4 + bm * D_MODEL * 2)
            + 2 * D_MODEL * D_MODEL * 2 + 4 * SLAB_ROWS * D_MODEL * 4 + 2 * 3 * wo_rows * D_MODEL * 6)
    const = lambda jj, i: (0, 0)
    rows = lambda jj, i: jnp.where(jj == 0, 0, i)
    w_slab = lambda jj, i: jnp.where(jj > last_group, n_w - 1, jnp.minimum(i, n_w - 1))
    wo_spec = pl.BlockSpec((wo_rows, D_MODEL), lambda jj, i: (jnp.minimum(jj * n_i + i, n_wo - 1), 0))
    wo_shape = jax.ShapeDtypeStruct((D_MODEL, D_MODEL), BF16)
    return pl.pallas_call(
        functools.partial(_proj_kernel, n_w, n_wo),
        grid=(N_COL_GROUPS + 1, n_i),
        in_specs=[
            pl.BlockSpec((bm, D_MODEL), lambda jj, i: (rows(jj, i), 0)),
            pl.BlockSpec((w_rows, D_MODEL), lambda jj, i: (w_slab(jj, i), jnp.minimum(jj, last_group))),
            pl.BlockSpec((1, D_MODEL), const),
            pl.BlockSpec((1, D_MODEL), const),
            pl.BlockSpec((2, D_MODEL), const),
            pl.BlockSpec((bm, R_QK_DIM), lambda jj, i: (i % pos_blocks, 0)),
            pl.BlockSpec((bm, R_QK_DIM), lambda jj, i: (i % pos_blocks, 0)),
            pl.BlockSpec((CHUNK, D_MODEL), const),
            wo_spec, wo_spec, wo_spec,
        ],
        out_specs=[pl.BlockSpec((bm, D_MODEL), lambda jj, i: (rows(jj, i), jnp.maximum(jj - 1, 0))),
                   wo_spec, wo_spec, wo_spec],
        out_shape=[jax.ShapeDtypeStruct((tokens, IN_WIDTH), BF16), wo_shape, wo_shape, wo_shape],
        scratch_shapes=[pltpu.VMEM((2, D_MODEL, D_MODEL), BF16)],
        compiler_params=pltpu.CompilerParams(
            dimension_semantics=("arbitrary", "arbitrary"),
            vmem_limit_bytes=min(vmem + SPILL_ALLOWANCE_BYTES, VMEM_CEILING)),
        name="proj",
    )(x2, w, ln_v_g, ln_v_b, b_gate2, cos_t, sin_t, dk_t, w_oa, w_ob, w_out)


def _mix_and_proj(chunk_decay, u_ref, vn_ref, az_ref, qk_ref, v_ref, rz_ref, ga_ref, gb_ref,
                  ws_ref, bsb_ref, woa_ref, wob_ref, m_ref, state, mix_to, proj_from):
    rows = u_ref.shape[0]
    heads = [(c, h) for c in range(rows // CHUNK) for h in range(R_HEADS)]
    n_tiles = D_MODEL // R_V_DIM
    assert len(heads) == 2 * n_tiles
    if mix_to is not None:
        a_scr, b_scr = mix_to
        t_idx = lax.broadcasted_iota(jnp.int32, (CHUNK, CHUNK), 0)
        s_idx = lax.broadcasted_iota(jnp.int32, (CHUNK, CHUNK), 1)
        causal = t_idx >= s_idx
        ws_causal = [jnp.where(causal, ws_ref[g], 0.0).astype(BF16) for g in range(A_GROUPS)]

    def qk(c, h):
        r = slice(c * CHUNK, (c + 1) * CHUNK)
        q = qk_ref[r, h * R_QK_DIM:(h + 1) * R_QK_DIM]
        k = qk_ref[r, R_QK_WIDTH + h * R_QK_DIM:R_QK_WIDTH + (h + 1) * R_QK_DIM]
        return lax.dot_general(q, k, (((1,), (1,)), ((), ())), preferred_element_type=F32)

    def spatial_gate(c, g):
        r = slice(c * CHUNK, (c + 1) * CHUNK)
        cs = slice(g * A_GROUP_DIM, (g + 1) * A_GROUP_DIM)
        sv = jnp.dot(ws_causal[g], vn_ref[r, cs], preferred_element_type=F32)
        bias = bsb_ref[g]
        sv = jnp.concatenate([sv[:, :CHUNK] + bias, sv[:, CHUNK:] + bias], axis=1)
        a = u_ref[r, cs].astype(F32) * sv * az_ref[r, cs].astype(F32)
        a_scr[r, cs] = a.astype(BF16)

    def retention(c, h, s):
        r = slice(c * CHUNK, (c + 1) * CHUNK)
        q = qk_ref[r, h * R_QK_DIM:(h + 1) * R_QK_DIM]
        k = qk_ref[r, R_QK_WIDTH + h * R_QK_DIM:R_QK_WIDTH + (h + 1) * R_QK_DIM]
        vs = slice(h * R_V_DIM, (h + 1) * R_V_DIM)
        v = v_ref[r, vs]
        s = jnp.where(causal, s, 0.0).astype(BF16)
        st = state[h]
        lhs = jnp.concatenate([s, q], axis=1)
        rhs = jnp.concatenate([v, st.astype(BF16)], axis=0)
        o = _standardize_rows(jnp.dot(lhs, rhs, preferred_element_type=F32))
        b_scr[r, vs] = (o * rz_ref[r, vs].astype(F32)).astype(BF16)
        kv = lax.dot_general(k, v, (((0,), (0,)), ((), ())), preferred_element_type=F32)
        state[h] = (st + kv) * chunk_decay[h]

    ya = None
    pending = [qk(*heads[0]), qk(*heads[1])] if mix_to is not None else None
    for idx, (c, h) in enumerate(heads):
        if mix_to is not None:
            if idx + 2 < len(heads):
                pending.append(qk(*heads[idx + 2]))
            spatial_gate(c, h)
            retention(c, h, pending.pop(0))
        if proj_from is not None:
            cols = slice((idx // 2) * R_V_DIM, (idx // 2 + 1) * R_V_DIM)
            if idx % 2 == 0:
                ya = jnp.dot(proj_from[0][...], woa_ref[:, cols], preferred_element_type=F32)
            else:
                yb = jnp.dot(proj_from[1][...], wob_ref[:, cols], preferred_element_type=F32)
                merged = ga_ref[:, cols].astype(F32) * ya + gb_ref[:, cols].astype(F32) * yb
                m_ref[:, cols] = merged.astype(BF16)


def _mixproj_kernel(chunk_decay, blocks_per_seq,
                    u_ref, vn_ref, az_ref, qk_ref, v_ref, rz_ref, ga_ref, gb_ref,
                    ws_ref, bsb_ref, woa_ref, wob_ref, m_ref,
                    state, a0, b0, a1, b1):
    t = pl.program_id(0)
    n_blocks = pl.num_programs(0) - 1

    @pl.when(t % blocks_per_seq == 0)
    def _():
        state[...] = jnp.zeros_like(state)

    step = functools.partial(_mix_and_proj, chunk_decay, u_ref, vn_ref, az_ref, qk_ref, v_ref, rz_ref,
                             ga_ref, gb_ref, ws_ref, bsb_ref, woa_ref, wob_ref, m_ref, state)

    @pl.when(t == 0)
    def _():
        step((a0, b0), None)

    @pl.when((t > 0) & (t < n_blocks) & (t % 2 == 1))
    def _():
        step((a1, b1), (a0, b0))

    @pl.when((t > 0) & (t < n_blocks) & (t % 2 == 0))
    def _():
        step((a0, b0), (a1, b1))

    @pl.when(t == n_blocks)
    def _():
        step(None, (a1, b1))


def _mixproj(p, ws, bsb, woa, wob, chunk_decay, seq):
    tokens = p.shape[0]
    tb = MIX_ROWS
    n_blocks = tokens // tb
    assert n_blocks % 2 == 0
    last = n_blocks - 1

    def cur(jcol):
        return pl.BlockSpec((tb, D_MODEL), lambda t: (jnp.minimum(t, last), jcol))

    def prev(jcol):
        return pl.BlockSpec((tb, D_MODEL), lambda t: (jnp.maximum(t - 1, 0), jcol))

    def whole(a):
        nd = a.ndim
        return pl.BlockSpec(a.shape, lambda t: (0,) * nd)

    wspec = pl.BlockSpec((D_MODEL, D_MODEL), lambda t: (0, 0), pipeline_mode=pl.Buffered(1))
    blk = tb * D_MODEL
    vmem = (2 * D_MODEL * D_MODEL * 2 + 2 * 8 * blk * 2 + 2 * blk * 2 + 4 * blk * 2
            + R_HEADS * R_QK_DIM * R_V_DIM * 4 + 4 * blk * 4 + 4 * (ws.size + bsb.size) * 4)
    return pl.pallas_call(
        functools.partial(_mixproj_kernel, chunk_decay, seq // tb),
        grid=(n_blocks + 1,),
        in_specs=[cur(0), cur(1), cur(2), cur(3), cur(4), cur(5), prev(6), prev(7),
                  whole(ws), whole(bsb), wspec, wspec],
        out_specs=pl.BlockSpec((tb, D_MODEL), lambda t: (jnp.maximum(t - 1, 0), 0)),
        out_shape=jax.ShapeDtypeStruct((tokens, D_MODEL), BF16),
        scratch_shapes=[pltpu.VMEM((R_HEADS, R_QK_DIM, R_V_DIM), F32)]
        + [pltpu.VMEM((tb, D_MODEL), BF16)] * 4,
        compiler_params=pltpu.CompilerParams(
            dimension_semantics=("arbitrary",),
            vmem_limit_bytes=min(vmem + SPILL_ALLOWANCE_BYTES, VMEM_CEILING)),
        name="mixproj",
    )(p, p, p, p, p, p, p, p, ws, bsb, woa, wob)


def _out_kernel(m_ref, x_ref, wout_ref, lng_ref, lnb_ref, o_ref):
    for q in range(m_ref.shape[0] // SLAB_ROWS):
        rows = slice(q * SLAB_ROWS, (q + 1) * SLAB_ROWS)
        out = jnp.dot(m_ref[rows, :], wout_ref[...], preferred_element_type=F32)
        y = _standardize_rows(np.float32(DEEPNORM_ALPHA) * x_ref[rows, :] + out)
        o_ref[rows, :] = y * lng_ref[...] + lnb_ref[...]


def _finish(m, x2, wout, ln_g, ln_b):
    tokens = x2.shape[0]
    tb = OUT_ROWS
    row = lambda i: (i, 0)
    const = lambda i: (0, 0)
    blk = tb * D_MODEL
    vmem = D_MODEL * D_MODEL * 2 + 2 * (blk * 2 + 2 * blk * 4) + 4 * SLAB_ROWS * D_MODEL * 4
    return pl.pallas_call(
        _out_kernel,
        grid=(tokens // tb,),
        in_specs=[
            pl.BlockSpec((tb, D_MODEL), row),
            pl.BlockSpec((tb, D_MODEL), row),
            pl.BlockSpec((D_MODEL, D_MODEL), const, pipeline_mode=pl.Buffered(1)),
            pl.BlockSpec((1, D_MODEL), const),
            pl.BlockSpec((1, D_MODEL), const),
        ],
        out_specs=pl.BlockSpec((tb, D_MODEL), row),
        out_shape=jax.ShapeDtypeStruct((tokens, D_MODEL), F32),
        compiler_params=pltpu.CompilerParams(
            dimension_semantics=("arbitrary",),
            vmem_limit_bytes=min(vmem + SPILL_ALLOWANCE_BYTES, VMEM_CEILING)),
        name="finish",
    )(m, x2, wout, ln_g, ln_b)


def _rotary_tables(seq):
    half = R_QK_DIM // 2
    freqs = ROPE_BASE ** (-jnp.arange(0, R_QK_DIM, 2, dtype=F32) / R_QK_DIM)
    ang = jnp.arange(seq, dtype=jnp.int32).astype(F32)[:, None] * freqs[None, :]
    cos, sin = jnp.cos(ang), jnp.sin(ang)
    assert cos.shape == (seq, half)
    return jnp.concatenate([cos, cos], axis=-1), jnp.concatenate([-sin, sin], axis=-1)


def _retention_constants():
    log_gamma = np.log1p(-np.exp2(-5.0 - np.arange(R_HEADS, dtype=np.float64)))
    pos = np.arange(CHUNK, dtype=np.float64) + 1.0
    q_fac = np.exp(log_gamma[None, :] * pos[:, None])
    k_fac = np.exp(-log_gamma[None, :] * pos[:, None]) * R_QK_DIM ** -0.5
    table = np.concatenate([np.repeat(q_fac, R_QK_DIM, axis=1), np.repeat(k_fac, R_QK_DIM, axis=1)], axis=1)
    chunk_decay = tuple(float(np.float32(v)) for v in np.exp(log_gamma * CHUNK))
    return jnp.asarray(table, F32), chunk_decay


def kernel(x, w_in, b_gate, ln_v_g, ln_v_b, w_s, b_s, w_oa, w_ob, w_out, ln_g, ln_b):
    batch, seq, d = x.shape
    assert d == D_MODEL and w_in.shape == (DEPTH, D_MODEL, IN_WIDTH)
    assert seq % PROJ_ROWS == 0 and seq % MIX_ROWS == 0 and (batch * seq) % OUT_ROWS == 0
    tokens = batch * seq
    x2 = x.reshape(tokens, D_MODEL)
    cos_t, sin_t = _rotary_tables(seq)
    dk_t, chunk_decay = _retention_constants()
    for l in range(DEPTH):
        p, woa, wob, wout = _project(x2, w_in[l], ln_v_g[l][None, :], ln_v_b[l][None, :],
                                     b_gate[l].reshape(2, D_MODEL), cos_t, sin_t, dk_t,
                                     w_oa[l], w_ob[l], w_out[l], seq)
        bsb = jnp.broadcast_to(b_s[l][:, :, None], (A_GROUPS, CHUNK, CHUNK))
        m = _mixproj(p, w_s[l], bsb, woa, wob, chunk_decay, seq)
        x2 = _finish(m, x2, wout, ln_g[l][None, :], ln_b[l][None, :])
    return x2.reshape(batch, seq, D_MODEL)
```
